```python
import math
import jax
import jax.numpy as jnp
from jax import lax
import numpy as np

D_MODEL = 1024
BATCH = 8
SEQ = 4096
DEPTH = 1

MEM_LEN = 256
MOBA_HEADS = 8
MOBA_HEAD_DIM = 64
MOBA_BLOCK = 256
MOBA_TOPK = 3
MOBA_Q_CHUNK = 32
RET_HEADS = 4
RET_QK_DIM = 128
RET_V_DIM = 128
RET_CHUNK = 128
ROPE_BASE = 10000.0
MEM_HEADS = 4
MEM_HEAD_DIM = 128
REL_BUCKETS = 32
REL_MAX_DIST = 2048
D_FF = 2816
CONV_WIDTH = 3
EPS = 1e-6
NEG_INF = -1e30

MOBA_W = MOBA_HEADS * MOBA_HEAD_DIM
RET_QK_W = RET_HEADS * RET_QK_DIM
RET_V_W = RET_HEADS * RET_V_DIM
MEM_W = MEM_HEADS * MEM_HEAD_DIM
IN_SIZES = (MOBA_W, MOBA_W, MOBA_W, RET_QK_W, RET_QK_W, RET_V_W, RET_V_W, MEM_W, D_MODEL, D_MODEL, D_MODEL)
IN_SPLIT_AT = tuple(int(s) for s in np.cumsum(IN_SIZES)[:-1])
D_IN = int(sum(IN_SIZES))

kernel_name = "hybrid_moba_retention_memory_block"


def rmsnorm(x, g):
    xf = x.astype(jnp.float32)
    y = xf * lax.rsqrt(jnp.mean(xf * xf, axis=-1, keepdims=True) + EPS)
    return (y * g.astype(jnp.float32)).astype(x.dtype)


def rel_bucket(dist):
    max_exact = REL_BUCKETS // 2
    d = jnp.maximum(dist, 0)
    df = jnp.maximum(d, 1).astype(jnp.float32)
    large = max_exact + (jnp.log(df / max_exact) / math.log(REL_MAX_DIST / max_exact)
                         * (REL_BUCKETS - max_exact)).astype(jnp.int32)
    large = jnp.minimum(large, REL_BUCKETS - 1)
    return jnp.where(d < max_exact, d, large)


def rotary(x, pos):
    half = x.shape[-1] // 2
    inv = ROPE_BASE ** (-jnp.arange(half, dtype=jnp.float32) / half)
    ang = pos.astype(jnp.float32)[:, None] * inv
    cos = jnp.cos(ang)[None, :, None, :]
    sin = jnp.sin(ang)[None, :, None, :]
    x1, x2 = x[..., :half], x[..., half:]
    return jnp.concatenate([x1 * cos - x2 * sin, x1 * sin + x2 * cos], axis=-1).astype(x.dtype)


def moba_attention(q, k, v, rel_bias):
    B, S, H, dh = q.shape
    nb = -(-S // MOBA_BLOCK)
    s_pad = nb * MOBA_BLOCK
    n_sel = min(MOBA_TOPK, nb)
    qc_len = MOBA_Q_CHUNK
    scale = dh ** -0.5
    q = q.transpose(0, 2, 1, 3)
    pad = ((0, 0), (0, 0), (0, s_pad - S), (0, 0))
    k = jnp.pad(k.transpose(0, 2, 1, 3), pad)
    v = jnp.pad(v.transpose(0, 2, 1, 3), pad)
    kb = k.reshape(B, H, nb, MOBA_BLOCK, dh)
    vb = v.reshape(B, H, nb, MOBA_BLOCK, dh)
    k_mean = jnp.mean(kb.astype(jnp.float32), axis=3)

    pos = jnp.arange(S)
    fully_past = jnp.arange(nb)[None, :] < (pos // MOBA_BLOCK)[:, None]
    gate = jnp.einsum('bhsd,bhnd->bhsn', q.astype(jnp.float32), k_mean)
    gate = jnp.where(fully_past, gate, NEG_INF)
    _, sel = lax.top_k(gate, n_sel)

    table = rel_bias.T.astype(jnp.float32)
    b_idx = jnp.arange(B)[:, None, None, None]
    h_idx = jnp.arange(H)[None, :, None, None]
    blk_off = jnp.arange(MOBA_BLOCK)

    def query_chunk(ci):
        c0 = ci * qc_len
        q_pos = c0 + jnp.arange(qc_len)
        blk = c0 // MOBA_BLOCK
        q_c = lax.dynamic_slice_in_dim(q, c0, qc_len, axis=2)
        sel_c = lax.dynamic_slice_in_dim(sel, c0, qc_len, axis=2)
        k_sel = kb[b_idx, h_idx, sel_c]
        v_sel = vb[b_idx, h_idx, sel_c]
        sel_pos = sel_c[..., None] * MOBA_BLOCK + blk_off
        s_sel = jnp.einsum('bhqd,bhqknd->bhqkn', q_c, k_sel).astype(jnp.float32) * scale
        s_sel = s_sel + table[h_idx[..., None], rel_bucket(q_pos[:, None, None] - sel_pos)]
        s_sel = jnp.where((sel_c < blk)[..., None], s_sel, NEG_INF)
        own_start = blk * MOBA_BLOCK
        k_own = lax.dynamic_slice_in_dim(k, own_start, MOBA_BLOCK, axis=2)
        v_own = lax.dynamic_slice_in_dim(v, own_start, MOBA_BLOCK, axis=2)
        d_own = q_pos[:, None] - (own_start + blk_off)[None, :]
        s_own = (jnp.einsum('bhqd,bhnd->bhqn', q_c, k_own).astype(jnp.float32) * scale
                 + table[:, rel_bucket(d_own)])
        s_own = jnp.where(d_own >= 0, s_own, NEG_INF)
        logits = jnp.concatenate([s_sel.reshape(B, H, qc_len, n_sel * MOBA_BLOCK), s_own], axis=-1)
        p = jax.nn.softmax(logits, axis=-1).astype(v.dtype)
        p_sel = p[..., :n_sel * MOBA_BLOCK].reshape(B, H, qc_len, n_sel, MOBA_BLOCK)
        p_own = p[..., n_sel * MOBA_BLOCK:]
        return (jnp.einsum('bhqkn,bhqkne->bhqe', p_sel, v_sel)
                + jnp.einsum('bhqn,bhne->bhqe', p_own, v_own))

    out = lax.map(query_chunk, jnp.arange(S // qc_len))
    return out.transpose(1, 0, 3, 2, 4).reshape(B, S, H * dh)


def retention(q, k, v, g, gn_gain):
    B, S, H, dk = q.shape
    dv = v.shape[-1]
    C = RET_CHUNK
    N = S // C
    pos = jnp.arange(S)
    q = rotary(q, pos)
    k = rotary(k, pos) * (dk ** -0.5)
    log_gamma = jnp.log1p(-jnp.power(2.0, -5.0 - jnp.arange(H, dtype=jnp.float32)))
    i = jnp.arange(C, dtype=jnp.float32)
    diff = i[:, None] - i[None, :]
    decay = jnp.where(diff >= 0, jnp.exp(log_gamma[:, None, None] * jnp.maximum(diff, 0.0)), 0.0)
    q_in = jnp.exp(log_gamma[:, None] * (i + 1.0))
    k_out = jnp.exp(log_gamma[:, None] * (C - 1.0 - i))
    chunk_decay = jnp.exp(log_gamma * C)

    def to_chunks(t):
        return t.reshape(B, N, C, H, t.shape[-1]).transpose(0, 3, 1, 2, 4)
    qc, kc, vc = to_chunks(q), to_chunks(k), to_chunks(v)
    scores = jnp.einsum('bhncd,bhnjd->bhncj', qc, kc) * decay[None, :, None]
    inner = jnp.einsum('bhncj,bhnje->bhnce', scores, vc)
    kv = jnp.einsum('bhnjd,bhnje->nbhde', kc * k_out[None, :, None, :, None], vc).astype(jnp.float32)

    def step(state, kv_n):
        return chunk_decay[None, :, None, None] * state + kv_n, state
    _, prev = lax.scan(step, jnp.zeros((B, H, dk, dv), jnp.float32), kv)
    cross = jnp.einsum('bhncd,nbhde->bhnce', qc * q_in[None, :, None, :, None], prev)
    y = (inner + cross).astype(jnp.float32).transpose(0, 2, 3, 1, 4).reshape(B, S, H, dv)
    mu = jnp.mean(y, axis=-1, keepdims=True)
    var = jnp.mean(jnp.square(y - mu), axis=-1, keepdims=True)
    yn = ((y - mu) * lax.rsqrt(var + EPS)).reshape(B, S, H * dv) * gn_gain.astype(jnp.float32)
    return (yn * jax.nn.silu(g.astype(jnp.float32))).astype(v.dtype)


def memory_attention(q, mem_n, w_mem_kv):
    B, S, H, dh = q.shape
    M = mem_n.shape[1]
    k, v = jnp.split(mem_n @ w_mem_kv, 2, axis=-1)
    k = k.reshape(B, M, H, dh)
    v = v.reshape(B, M, H, dh)
    s = jnp.einsum('bshd,bmhd->bhsm', q, k).astype(jnp.float32) * (dh ** -0.5)
    p = jax.nn.softmax(s, axis=-1).astype(v.dtype)
    return jnp.einsum('bhsm,bmhd->bshd', p, v).reshape(B, S, H * dh)


def conv_ffn(x, w_up, conv_w, conv_b, w_down):
    h = x @ w_up
    c = h.shape[-1]
    h = lax.conv_general_dilated(
        h, conv_w.reshape(CONV_WIDTH, 1, c).astype(h.dtype), window_strides=(1,),
        padding=[(CONV_WIDTH - 1, 0)], dimension_numbers=('NWC', 'WIO', 'NWC'),
        feature_group_count=c) + conv_b
    gate, up = jnp.split(h, 2, axis=-1)
    return (jax.nn.gelu(gate, approximate=False) * up) @ w_down


def setup_inputs(seed: int = 0) -> dict:
    key = jax.random.key(seed)
    ks = jax.random.split(key, 18)
    f32 = jnp.float32
    L = DEPTH

    def dense(k, shape, fan_in):
        return jax.random.normal(k, shape, f32) * fan_in ** -0.5

    def gain(k, shape):
        return 1.0 + 0.02 * jax.random.normal(k, shape, f32)

    return {
        "x": jax.random.normal(ks[0], (BATCH, SEQ, D_MODEL), f32),
        "mem": jax.random.normal(ks[1], (BATCH, MEM_LEN, D_MODEL), f32),
        "g_mix": gain(ks[2], (L, D_MODEL)),
        "w_in": dense(ks[3], (L, D_MODEL, D_IN), D_MODEL),
        "rel_bias": 0.5 * jax.random.normal(ks[4], (REL_BUCKETS, MOBA_HEADS), f32),
        "ret_gn_gain": gain(ks[5], (L, RET_V_W)),
        "g_mem": gain(ks[6], (L, D_MODEL)),
        "w_mem_kv": dense(ks[7], (L, D_MODEL, 2 * MEM_W), D_MODEL),
        "w_br_attn": dense(ks[8], (L, MOBA_W, D_MODEL), MOBA_W),
        "w_br_ret": dense(ks[9], (L, RET_V_W, D_MODEL), RET_V_W),
        "w_br_mem": dense(ks[10], (L, MEM_W, D_MODEL), MEM_W),
        "w_out": dense(ks[11], (L, D_MODEL, D_MODEL), D_MODEL),
        "g_ffn": gain(ks[12], (L, D_MODEL)),
        "w_up": dense(ks[13], (L, D_MODEL, 2 * D_FF), D_MODEL),
        "conv_w": dense(ks[14], (L, CONV_WIDTH, 2 * D_FF), CONV_WIDTH),
        "conv_b": 0.02 * jax.random.normal(ks[15], (L, 2 * D_FF), f32),
        "w_down": dense(ks[16], (L, D_FF, D_MODEL), D_FF),
        "g_final": gain(ks[17], (D_MODEL,)),
    }


def reference(x, mem, g_mix, w_in, rel_bias, ret_gn_gain, g_mem, w_mem_kv, w_br_attn, w_br_ret,
              w_br_mem, w_out, g_ffn, w_up, conv_w, conv_b, w_down, g_final):
    B, S, _ = x.shape
    h = x
    for l in range(DEPTH):
        n = rmsnorm(h, g_mix[l])
        (q_a, k_a, v_a, q_r, k_r, v_r, g_r, q_m, z_a, z_r, z_m) = jnp.split(n @ w_in[l], IN_SPLIT_AT, axis=-1)
        y_a = moba_attention(q_a.reshape(B, S, MOBA_HEADS, MOBA_HEAD_DIM),
                             k_a.reshape(B, S, MOBA_HEADS, MOBA_HEAD_DIM),
                             v_a.reshape(B, S, MOBA_HEADS, MOBA_HEAD_DIM), rel_bias)
        y_r = retention(q_r.reshape(B, S, RET_HEADS, RET_QK_DIM),
                        k_r.reshape(B, S, RET_HEADS, RET_QK_DIM),
                        v_r.reshape(B, S, RET_HEADS, RET_V_DIM), g_r, ret_gn_gain[l])
        y_m = memory_attention(q_m.reshape(B, S, MEM_HEADS, MEM_HEAD_DIM),
                               rmsnorm(mem, g_mem[l]), w_mem_kv[l])
        merged = (jax.nn.sigmoid(z_a) * (y_a @ w_br_attn[l])
                  + jax.nn.sigmoid(z_r) * (y_r @ w_br_ret[l])
                  + jax.nn.sigmoid(z_m) * (y_m @ w_br_mem[l]))
        h = h + merged @ w_out[l]
        h = h + conv_ffn(rmsnorm(h, g_ffn[l]), w_up[l], conv_w[l], conv_b[l], w_down[l])
    return rmsnorm(h, g_final)
```

```python
import functools
import math

import jax
import jax.numpy as jnp
from jax import lax
from jax.experimental import pallas as pl
from jax.experimental.pallas import tpu as pltpu

F32 = jnp.float32
BF16 = jnp.bfloat16

D_MODEL = 1024
MEM_LEN = 256
MOBA_HEADS = 8
MOBA_HEAD_DIM = 64
MOBA_BLOCK = 256
MOBA_TOPK = 3
RET_HEADS = 4
RET_DIM = 128
RET_CHUNK = 128
ROPE_BASE = 10000.0
MEM_HEADS = 4
MEM_HEAD_DIM = 128
REL_BUCKETS = 32
REL_MAX_DIST = 2048
D_FF = 2816
CONV_WIDTH = 3
EPS = 1e-6
NEG_INF = -1e30

MOBA_W = MOBA_HEADS * MOBA_HEAD_DIM
RET_W = RET_HEADS * RET_DIM
MEM_W = MEM_HEADS * MEM_HEAD_DIM
D_IN = 3 * MOBA_W + 4 * RET_W + MEM_W + 3 * D_MODEL

V7X_LANES = 128
V7X_SUBLANES = 8
V7X_VMEM_BYTES = 64 * 1024 * 1024
V7X_VMEM_RESERVE_BYTES = 6 * 1024 * 1024

_QA_BLK = 0
_KA_BLK = MOBA_W // V7X_LANES
_VA_BLK = 2 * MOBA_W // V7X_LANES
_QR_BLK = 3 * MOBA_W // V7X_LANES
_KR_BLK = _QR_BLK + RET_W // V7X_LANES
_VR_BLK = _KR_BLK + RET_W // V7X_LANES
_GR_BLK = _VR_BLK + RET_W // V7X_LANES
_QM_COL = 3 * MOBA_W + 4 * RET_W
_Z_COL = _QM_COL + MEM_W

ROW_TILE = 512
PROJ_COL_CHUNK = 512
RET_ROWS = 512
FFN_COL_CHUNK = 256
CONV_HALO = V7X_SUBLANES


def _compiler_params(semantics, block_bytes, scratch_bytes=0):
  need = 2 * block_bytes + scratch_bytes + 12 * 1024 * 1024
  limit = min(need, V7X_VMEM_BYTES - V7X_VMEM_RESERVE_BYTES)
  return pltpu.CompilerParams(dimension_semantics=semantics, vmem_limit_bytes=limit)


def _nbytes(shape, dtype):
  return math.prod(shape) * jnp.dtype(dtype).itemsize


def _rmsnorm(x, g):
  ms = jnp.mean(x * x, axis=-1, keepdims=True)
  return x * lax.rsqrt(ms + EPS) * g


def _dot(a, b):
  return jnp.dot(a, b, preferred_element_type=F32)


def _dot_nt(a, b):
  return lax.dot_general(a, b, (((1,), (1,)), ((), ())), preferred_element_type=F32)


def _dot_tn(a, b):
  return lax.dot_general(a, b, (((0,), (0,)), ((), ())), preferred_element_type=F32)


def _in_proj_kernel(x_ref, g_ref, w_ref, o_ref):
  n = _rmsnorm(x_ref[...], g_ref[...]).astype(BF16)
  for c in range(D_IN // PROJ_COL_CHUNK):
    cols = slice(c * PROJ_COL_CHUNK, (c + 1) * PROJ_COL_CHUNK)
    o_ref[:, cols] = _dot(n, w_ref[:, cols]).astype(BF16)


def _in_proj(x2d, g, w):
  t = x2d.shape[0]
  blocks = (_nbytes((ROW_TILE, D_MODEL), F32) + _nbytes((D_MODEL, D_IN), BF16)
            + _nbytes((ROW_TILE, D_IN), BF16))
  return pl.pallas_call(
      _in_proj_kernel,
      out_shape=jax.ShapeDtypeStruct((t, D_IN), BF16),
      grid=(t // ROW_TILE,),
      in_specs=[
          pl.BlockSpec((ROW_TILE, D_MODEL), lambda i: (i, 0)),
          pl.BlockSpec((1, D_MODEL), lambda i: (0, 0)),
          pl.BlockSpec((D_MODEL, D_IN), lambda i: (0, 0)),
      ],
      out_specs=pl.BlockSpec((ROW_TILE, D_IN), lambda i: (i, 0)),
      compiler_params=_compiler_params(("parallel",), blocks),
      name="in_proj",
  )(x2d, g, w)


def _rel_bucket(dist):
  max_exact = REL_BUCKETS // 2
  d = jnp.maximum(dist, 0)
  df = jnp.maximum(d, 1).astype(F32)
  large = max_exact + (jnp.log(df / max_exact) / math.log(REL_MAX_DIST / max_exact)
                       * (REL_BUCKETS - max_exact)).astype(jnp.int32)
  large = jnp.minimum(large, REL_BUCKETS - 1)
  return jnp.where(d < max_exact, d, large)


def _bias_kernel(tab_ref, bkt_ref, o_ref):
  h = pl.program_id(0)
  delta = pl.program_id(1)
  bkt = bkt_ref[0]
  w = jnp.zeros(bkt.shape, F32)
  for b in range(REL_BUCKETS):
    w = jnp.where(bkt == b, tab_ref[b, h], w)
  wide = jnp.broadcast_to(w, (MOBA_BLOCK, 2 * MOBA_BLOCK))
  tile = pltpu.roll(wide, MOBA_BLOCK + 1, 1, stride=1, stride_axis=0)[:, :MOBA_BLOCK]
  c = lax.broadcasted_iota(jnp.int32, tile.shape, 0)
  r = lax.broadcasted_iota(jnp.int32, tile.shape, 1)
  o_ref[0, 0] = jnp.where((delta > 0) | (r >= c), tile, NEG_INF)


def _bias_tiles(rel_bias, nb):
  m = jnp.arange(2 * MOBA_BLOCK, dtype=jnp.int32)
  dist = jnp.arange(nb, dtype=jnp.int32)[:, None] * MOBA_BLOCK + m[None, :] - (MOBA_BLOCK - 1)
  bkt = _rel_bucket(dist).reshape(nb, 1, 2 * MOBA_BLOCK)
  return pl.pallas_call(
      _bias_kernel,
      out_shape=jax.ShapeDtypeStruct((MOBA_HEADS, nb, MOBA_BLOCK, MOBA_BLOCK), F32),
      grid=(MOBA_HEADS, nb),
      in_specs=[
          pl.BlockSpec(memory_space=pltpu.SMEM),
          pl.BlockSpec((1, 1, 2 * MOBA_BLOCK), lambda h, d: (d, 0, 0)),
      ],
      out_specs=pl.BlockSpec((1, 1, MOBA_BLOCK, MOBA_BLOCK), lambda h, d: (h, d, 0, 0)),
      compiler_params=_compiler_params(
          ("parallel", "parallel"), _nbytes((MOBA_BLOCK, MOBA_BLOCK), F32)),
      name="bias_tiles",
  )(rel_bias, bkt)


def _moba_kernel(q_ref, k_ref, v_ref, bias_ref, o_ref, kmean_ref, pen_ref, *, nb):
  i = pl.program_id(2)
  blk = MOBA_BLOCK
  dh = MOBA_HEAD_DIM

  @pl.when(i == 0)
  def _():
    for j in range(nb):
      kb = k_ref[0, j * blk:(j + 1) * blk, :].astype(F32)
      kmean_ref[j:j + 1, :] = jnp.mean(kb, axis=0, keepdims=True)

  own = pl.multiple_of(i * blk, blk)
  outs = []
  for hh in range(V7X_LANES // dh):
    lanes = slice(hh * dh, (hh + 1) * dh)
    q = q_ref[0, :, lanes]

    gate = lax.dot_general(kmean_ref[:, lanes], q.astype(F32), (((1,), (1,)), ((), ())),
                           precision=lax.Precision.HIGHEST, preferred_element_type=F32)
    jidx = lax.broadcasted_iota(jnp.int32, gate.shape, 0)
    past = jidx < i
    gate = jnp.where(past, gate, NEG_INF)
    rank = jnp.zeros(gate.shape, F32)
    for jp in range(nb):
      row = gate[jp:jp + 1, :]
      ahead = jnp.where(row > gate, 1.0, jnp.where((row == gate) & (jidx > jp), 1.0, 0.0))
      rank = rank + ahead
    pen_ref[hh] = jnp.where(past & (rank < MOBA_TOPK), 0.0, NEG_INF)

    qs = (q.astype(F32) * (dh ** -0.5)).astype(BF16)

    def scores(start, delta, qs=qs, lanes=lanes, hh=hh):
      kj = k_ref[0, pl.ds(start, blk), lanes]
      return _dot_nt(kj, qs) + bias_ref[hh, delta]

    def pv(start, p, lanes=lanes):
      vj = v_ref[0, pl.ds(start, blk), lanes]
      return _dot_tn(vj, p.astype(BF16))

    s = scores(own, 0)
    m0 = jnp.max(s, axis=0, keepdims=True)
    p = jnp.exp(s - m0)
    l0 = jnp.sum(p, axis=0, keepdims=True)
    acc0 = pv(own, p)

    def body(j, carry, scores=scores, pv=pv, hh=hh):
      m, l, acc = carry
      start = pl.multiple_of(j * blk, blk)
      s = scores(start, i - j) + pen_ref[hh, pl.ds(j, 1), :]
      m_new = jnp.maximum(m, jnp.max(s, axis=0, keepdims=True))
      alpha = jnp.exp(m - m_new)
      p = jnp.exp(s - m_new)
      l = alpha * l + jnp.sum(p, axis=0, keepdims=True)
      acc = alpha * acc + pv(start, p)
      return m_new, l, acc

    _, l, acc = lax.fori_loop(0, i, body, (m0, l0, acc0))
    outs.append((acc / l).T)
  o_ref[0] = jnp.concatenate(outs, axis=1).astype(o_ref.dtype)


def _moba(proj3, bias):
  b, s, _ = proj3.shape
  nb = s // MOBA_BLOCK
  pairs = MOBA_W // V7X_LANES
  per_pair = V7X_LANES // MOBA_HEAD_DIM
  blocks = (2 * _nbytes((MOBA_BLOCK, V7X_LANES), BF16) + 2 * _nbytes((s, V7X_LANES), BF16)
            + _nbytes((per_pair, nb, MOBA_BLOCK, MOBA_BLOCK), F32))
  scratch = _nbytes((nb, V7X_LANES), F32) + _nbytes((per_pair, nb, MOBA_BLOCK), F32)
  return pl.pallas_call(
      functools.partial(_moba_kernel, nb=nb),
      out_shape=jax.ShapeDtypeStruct((b, s, MOBA_W), BF16),
      grid=(b, pairs, nb),
      in_specs=[
          pl.BlockSpec((1, MOBA_BLOCK, V7X_LANES), lambda bi, hp, i: (bi, i, _QA_BLK + hp)),
          pl.BlockSpec((1, s, V7X_LANES), lambda bi, hp, i: (bi, 0, _KA_BLK + hp)),
          pl.BlockSpec((1, s, V7X_LANES), lambda bi, hp, i: (bi, 0, _VA_BLK + hp)),
          pl.BlockSpec((per_pair, nb, MOBA_BLOCK, MOBA_BLOCK), lambda bi, hp, i: (hp, 0, 0, 0)),
      ],
      out_specs=pl.BlockSpec((1, MOBA_BLOCK, V7X_LANES), lambda bi, hp, i: (bi, i, hp)),
      scratch_shapes=[
          pltpu.VMEM((nb, V7X_LANES), F32),
          pltpu.VMEM((per_pair, nb, MOBA_BLOCK), F32),
      ],
      compiler_params=_compiler_params(("parallel", "parallel", "arbitrary"), blocks, scratch),
      name="moba",
  )(proj3, proj3, proj3, bias)


def _retention_kernel(q_ref, k_ref, v_ref, g_ref, cos_ref, sin_ref, decay_ref, qin_ref, kout_ref,
                      cdec_ref, gain_ref, o_ref, state_ref):
  c_len = RET_CHUNK
  half = RET_DIM // 2

  @pl.when(pl.program_id(2) == 0)
  def _():
    state_ref[...] = jnp.zeros(state_ref.shape, F32)

  for c in range(RET_ROWS // c_len):
    rows = slice(c * c_len, (c + 1) * c_len)
    cos = cos_ref[rows, :]
    sin = sin_ref[rows, :]
    q = q_ref[0, rows, :].astype(F32)
    k = k_ref[0, rows, :].astype(F32)
    v = v_ref[0, rows, :]
    qr = q * cos + pltpu.roll(q, half, 1) * sin
    kr = (k * cos + pltpu.roll(k, half, 1) * sin) * (RET_DIM ** -0.5)
    state = state_ref[...]
    sc = _dot_nt(qr.astype(BF16), kr.astype(BF16)) * decay_ref[0]
    inner = _dot(sc.astype(BF16), v)
    cross = _dot((qr * qin_ref[0]).astype(BF16), state.astype(BF16))
    kv = _dot_tn((kr * kout_ref[0]).astype(BF16), v)
    state_ref[...] = cdec_ref[0] * state + kv
    y = inner + cross
    mu = jnp.mean(y, axis=-1, keepdims=True)
    yc = y - mu
    var = jnp.mean(yc * yc, axis=-1, keepdims=True)
    yn = yc * lax.rsqrt(var + EPS) * gain_ref[...]
    g = g_ref[0, rows, :].astype(F32)
    o_ref[0, rows, :] = (yn * (g * jax.nn.sigmoid(g))).astype(o_ref.dtype)


def _retention(proj3, gn_gain):
  b, s, _ = proj3.shape
  c_len = RET_CHUNK
  half = RET_DIM // 2
  pos = jnp.arange(s, dtype=F32)
  inv = ROPE_BASE ** (-jnp.arange(half, dtype=F32) / half)
  ang = pos[:, None] * inv
  cos = jnp.concatenate([jnp.cos(ang), jnp.cos(ang)], axis=-1)
  sin = jnp.concatenate([-jnp.sin(ang), jnp.sin(ang)], axis=-1)
  log_gamma = jnp.log1p(-jnp.power(2.0, -5.0 - jnp.arange(RET_HEADS, dtype=F32)))
  idx = jnp.arange(c_len, dtype=F32)
  diff = idx[:, None] - idx[None, :]
  decay = jnp.where(diff >= 0, jnp.exp(log_gamma[:, None, None] * jnp.maximum(diff, 0.0)), 0.0)
  lanes = (RET_HEADS, c_len, RET_DIM)
  q_in = jnp.broadcast_to(jnp.exp(log_gamma[:, None] * (idx + 1.0))[:, :, None], lanes)
  k_out = jnp.broadcast_to(jnp.exp(log_gamma[:, None] * (c_len - 1.0 - idx))[:, :, None], lanes)
  cdec = jnp.broadcast_to(jnp.exp(log_gamma * c_len)[:, None, None], (RET_HEADS, 1, RET_DIM))

  row_blk = lambda off: pl.BlockSpec((1, RET_ROWS, RET_DIM), lambda bi, h, t: (bi, t, off + h))
  head_blk = lambda shape: pl.BlockSpec((1,) + shape, lambda bi, h, t: (h, 0, 0))
  blocks = (5 * _nbytes((RET_ROWS, RET_DIM), BF16) + 2 * _nbytes((RET_ROWS, RET_DIM), F32)
            + 3 * _nbytes((c_len, RET_DIM), F32))
  return pl.pallas_call(
      _retention_kernel,
      out_shape=jax.ShapeDtypeStruct((b, s, RET_W), BF16),
      grid=(b, RET_HEADS, s // RET_ROWS),
      in_specs=[
          row_blk(_QR_BLK), row_blk(_KR_BLK), row_blk(_VR_BLK), row_blk(_GR_BLK),
          pl.BlockSpec((RET_ROWS, RET_DIM), lambda bi, h, t: (t, 0)),
          pl.BlockSpec((RET_ROWS, RET_DIM), lambda bi, h, t: (t, 0)),
          head_blk((c_len, c_len)), head_blk((c_len, RET_DIM)), head_blk((c_len, RET_DIM)),
          head_blk((1, RET_DIM)),
          pl.BlockSpec((1, RET_DIM), lambda bi, h, t: (0, h)),
      ],
      out_specs=pl.BlockSpec((1, RET_ROWS, RET_DIM), lambda bi, h, t: (bi, t, h)),
      scratch_shapes=[pltpu.VMEM((RET_DIM, RET_DIM), F32)],
      compiler_params=_compiler_params(("parallel", "parallel", "arbitrary"), blocks,
                                       _nbytes((RET_DIM, RET_DIM), F32)),
      name="retention",
  )(proj3, proj3, proj3, proj3, cos, sin, decay, q_in, k_out, cdec, gn_gain)


def _mem_kv_kernel(m_ref, g_ref, w_ref, o_ref):
  n = _rmsnorm(m_ref[0], g_ref[...]).astype(BF16)
  o_ref[0] = _dot(n, w_ref[...]).astype(BF16)


def _mem_kv(mem, g, w):
  b = mem.shape[0]
  blocks = (_nbytes((MEM_LEN, D_MODEL), F32) + _nbytes((D_MODEL, 2 * MEM_W), BF16)
            + _nbytes((MEM_LEN, 2 * MEM_W), BF16))
  return pl.pallas_call(
      _mem_kv_kernel,
      out_shape=jax.ShapeDtypeStruct((b, MEM_LEN, 2 * MEM_W), BF16),
      grid=(b,),
      in_specs=[
          pl.BlockSpec((1, MEM_LEN, D_MODEL), lambda i: (i, 0, 0)),
          pl.BlockSpec((1, D_MODEL), lambda i: (0, 0)),
          pl.BlockSpec((D_MODEL, 2 * MEM_W), lambda i: (0, 0)),
      ],
      out_specs=pl.BlockSpec((1, MEM_LEN, 2 * MEM_W), lambda i: (i, 0, 0)),
      compiler_params=_compiler_params(("parallel",), blocks),
      name="mem_kv",
  )(mem, g, w)


def _mem_attn_kernel(q_ref, kv_ref, o_ref):
  dh = MEM_HEAD_DIM
  outs = []
  for h in range(MEM_HEADS):
    q = q_ref[0, :, h * dh:(h + 1) * dh]
    k = kv_ref[0, :, h * dh:(h + 1) * dh]
    v = kv_ref[0, :, MEM_W + h * dh:MEM_W + (h + 1) * dh]
    s = _dot_nt(q, k) * (dh ** -0.5)
    p = jnp.exp(s - jnp.max(s, axis=-1, keepdims=True))
    l = jnp.sum(p, axis=-1, keepdims=True)
    outs.append(_dot(p.astype(BF16), v) / l)
  o_ref[0] = jnp.concatenate(outs, axis=1).astype(o_ref.dtype)


def _mem_attn(proj3, kv):
  b, s, _ = proj3.shape
  blocks = (2 * _nbytes((ROW_TILE, MEM_W), BF16) + _nbytes((MEM_LEN, 2 * MEM_W), BF16))
  return pl.pallas_call(
      _mem_attn_kernel,
      out_shape=jax.ShapeDtypeStruct((b, s, MEM_W), BF16),
      grid=(b, s // ROW_TILE),
      in_specs=[
          pl.BlockSpec((1, ROW_TILE, MEM_W), lambda bi, t: (bi, t, _QM_COL // MEM_W)),
          pl.BlockSpec((1, MEM_LEN, 2 * MEM_W), lambda bi, t: (bi, 0, 0)),
      ],
      out_specs=pl.BlockSpec((1, ROW_TILE, MEM_W), lambda bi, t: (bi, t, 0)),
      compiler_params=_compiler_params(("parallel", "parallel"), blocks),
      name="mem_attn",
  )(proj3, kv)


def _merge_kernel(x_ref, ya_ref, yr_ref, ym_ref, za_ref, zr_ref, zm_ref, wa_ref, wr_ref, wm_ref,
                  wo_ref, o_ref):
  def branch(y_ref, z_ref, w_ref):
    return jax.nn.sigmoid(z_ref[...].astype(F32)) * _dot(y_ref[...], w_ref[...])

  merged = (branch(ya_ref, za_ref, wa_ref) + branch(yr_ref, zr_ref, wr_ref)
            + branch(ym_ref, zm_ref, wm_ref))
  o_ref[...] = x_ref[...] + _dot(merged.astype(BF16), wo_ref[...])


def _merge(x2d, ya, yr, ym, proj2d, wa, wr, wm, wo):
  t = x2d.shape[0]
  zblk = _Z_COL // D_MODEL
  row = lambda w, col=0: pl.BlockSpec((ROW_TILE, w), lambda i: (i, col))
  full = lambda shape: pl.BlockSpec(shape, lambda i: (0, 0))
  blocks = (2 * _nbytes((ROW_TILE, D_MODEL), F32) + 3 * _nbytes((ROW_TILE, MOBA_W), BF16)
            + 3 * _nbytes((ROW_TILE, D_MODEL), BF16) + 3 * _nbytes((MOBA_W, D_MODEL), BF16)
            + _nbytes((D_MODEL, D_MODEL), BF16))
  return pl.pallas_call(
      _merge_kernel,
      out_shape=jax.ShapeDtypeStruct((t, D_MODEL), F32),
      grid=(t // ROW_TILE,),
      in_specs=[
          row(D_MODEL), row(MOBA_W), row(RET_W), row(MEM_W),
          row(D_MODEL, zblk), row(D_MODEL, zblk + 1), row(D_MODEL, zblk + 2),
          full((MOBA_W, D_MODEL)), full((RET_W, D_MODEL)), full((MEM_W, D_MODEL)),
          full((D_MODEL, D_MODEL)),
      ],
      out_specs=row(D_MODEL),
      compiler_params=_compiler_params(("parallel",), blocks),
      name="merge",
  )(x2d, ya, yr, ym, proj2d, proj2d, proj2d, wa, wr, wm, wo)


def _ffn_kernel(h_ref, gf_ref, wup_ref, cw_ref, cb_ref, wdn_ref, gfin_ref, o_ref, buf_ref,
                carry_ref):
  tm = ROW_TILE
  halo = CONV_HALO
  cw = FFN_COL_CHUNK
  t = pl.program_id(1)

  @pl.when(t == 0)
  def _():
    carry_ref[...] = jnp.zeros(carry_ref.shape, F32)

  h1 = h_ref[0]
  n = _rmsnorm(h1, gf_ref[...]).astype(BF16)
  acc = jnp.zeros((tm, D_MODEL), F32)
  for c in range(D_FF // cw):
    halves = []
    for part in range(2):
      col = part * D_FF + c * cw
      cols = slice(col, col + cw)
      slot = 2 * c + part
      buf_ref[0:halo, :] = carry_ref[slot]
      buf_ref[halo:halo + tm, :] = _dot(n, wup_ref[:, cols])
      carry_ref[slot] = buf_ref[tm:tm + halo, :]
      conv = cb_ref[:, cols]
      for tap in range(CONV_WIDTH):
        shift = CONV_WIDTH - 1 - tap
        conv = conv + cw_ref[tap:tap + 1, cols] * buf_ref[halo - shift:halo - shift + tm, :]
      halves.append(conv)
    gate, up = halves
    act = 0.5 * gate * (1.0 + lax.erf(gate * math.sqrt(0.5))) * up
    acc = acc + _dot(act.astype(BF16), wdn_ref[c * cw:(c + 1) * cw, :])
  o_ref[0] = _rmsnorm(h1 + acc, gfin_ref[...])


def _ffn(h3, g_ffn, w_up, conv_w, conv_b, w_down, g_final):
  b, s, _ = h3.shape
  full = lambda shape: pl.BlockSpec(shape, lambda bi, t: (0, 0))
  blocks = (2 * _nbytes((ROW_TILE, D_MODEL), F32) + _nbytes((D_MODEL, 2 * D_FF), BF16)
            + _nbytes((D_FF, D_MODEL), BF16) + _nbytes((CONV_WIDTH + 1, 2 * D_FF), F32))
  slots = 2 * D_FF // FFN_COL_CHUNK
  scratch = (_nbytes((ROW_TILE + CONV_HALO, FFN_COL_CHUNK), F32)
             + _nbytes((slots, CONV_HALO, FFN_COL_CHUNK), F32))
  return pl.pallas_call(
      _ffn_kernel,
      out_shape=jax.ShapeDtypeStruct((b, s, D_MODEL), F32),
      grid=(b, s // ROW_TILE),
      in_specs=[
          pl.BlockSpec((1, ROW_TILE, D_MODEL), lambda bi, t: (bi, t, 0)),
          full((1, D_MODEL)),
          full((D_MODEL, 2 * D_FF)),
          full((CONV_WIDTH, 2 * D_FF)),
          full((1, 2 * D_FF)),
          full((D_FF, D_MODEL)),
          full((1, D_MODEL)),
      ],
      out_specs=pl.BlockSpec((1, ROW_TILE, D_MODEL), lambda bi, t: (bi, t, 0)),
      scratch_shapes=[
          pltpu.VMEM((ROW_TILE + CONV_HALO, FFN_COL_CHUNK), F32),
          pltpu.VMEM((slots, CONV_HALO, FFN_COL_CHUNK), F32),
      ],
      compiler_params=_compiler_params(("parallel", "arbitrary"), blocks, scratch),
      name="ffn",
  )(h3, g_ffn, w_up, conv_w, conv_b, w_down, g_final)


def kernel(x, mem, g_mix, w_in, rel_bias, ret_gn_gain, g_mem, w_mem_kv, w_br_attn, w_br_ret,
           w_br_mem, w_out, g_ffn, w_up, conv_w, conv_b, w_down, g_final):
  b, s, d = x.shape
  assert d == D_MODEL and mem.shape == (b, MEM_LEN, D_MODEL)
  assert s % MOBA_BLOCK == 0 and s % ROW_TILE == 0 and s % RET_ROWS == 0
  assert g_mix.shape[0] == 1, "single-layer block"
  h = x.reshape(b * s, d)
  bias = _bias_tiles(rel_bias, s // MOBA_BLOCK)
  proj = _in_proj(h, g_mix, w_in[0].astype(BF16))
  proj3 = proj.reshape(b, s, D_IN)
  y_a = _moba(proj3, bias)
  y_r = _retention(proj3, ret_gn_gain)
  kv = _mem_kv(mem, g_mem, w_mem_kv[0].astype(BF16))
  y_m = _mem_attn(proj3, kv)
  h1 = _merge(h, y_a.reshape(b * s, MOBA_W), y_r.reshape(b * s, RET_W), y_m.reshape(b * s, MEM_W),
              proj, w_br_attn[0].astype(BF16), w_br_ret[0].astype(BF16),
              w_br_mem[0].astype(BF16), w_out[0].astype(BF16))
  return _ffn(h1.reshape(b, s, d), g_ffn, w_up[0].astype(BF16), conv_w[0], conv_b,
              w_down[0].astype(BF16), g_final.reshape(1, d))
```

```python
import functools
import math

import jax
import jax.numpy as jnp
from jax import lax
from jax.experimental import pallas as pl
from jax.experimental.pallas import tpu as pltpu

F32 = jnp.float32
BF16 = jnp.bfloat16

D_MODEL = 1024
MEM_LEN = 256
MOBA_HEADS = 8
MOBA_HEAD_DIM = 64
MOBA_BLOCK = 256
MOBA_TOPK = 3
RET_HEADS = 4
RET_DIM = 128
RET_CHUNK = 128
ROPE_BASE = 10000.0
MEM_HEADS = 4
MEM_HEAD_DIM = 128
REL_BUCKETS = 32
REL_MAX_DIST = 2048
D_FF = 2816
CONV_WIDTH = 3
EPS = 1e-6
NEG_INF = -1e30

MOBA_W = MOBA_HEADS * MOBA_HEAD_DIM
RET_W = RET_HEADS * RET_DIM
MEM_W = MEM_HEADS * MEM_HEAD_DIM
D_IN = 3 * MOBA_W + 4 * RET_W + MEM_W + 3 * D_MODEL

V7X_LANES = 128
V7X_SUBLANES = 8
V7X_VMEM_BYTES = 64 * 1024 * 1024
V7X_VMEM_RESERVE_BYTES = 6 * 1024 * 1024

_QA_BLK = 0
_KA_BLK = MOBA_W // V7X_LANES
_VA_BLK = 2 * MOBA_W // V7X_LANES
_QR_BLK = 3 * MOBA_W // V7X_LANES
_KR_BLK = _QR_BLK + RET_W // V7X_LANES
_VR_BLK = _KR_BLK + RET_W // V7X_LANES
_GR_BLK = _VR_BLK + RET_W // V7X_LANES
_QM_COL = 3 * MOBA_W + 4 * RET_W
_Z_COL = _QM_COL + MEM_W

ROW_TILE = 512
PROJ_COL_CHUNK = 512
RET_ROWS = 512
FFN_COL_CHUNK = 256
MOBA_STEP_BLOCKS = 4
CONV_HALO = V7X_SUBLANES


def _compiler_params(semantics, block_bytes, scratch_bytes=0):
  need = 2 * block_bytes + scratch_bytes + 12 * 1024 * 1024
  limit = min(need, V7X_VMEM_BYTES - V7X_VMEM_RESERVE_BYTES)
  return pltpu.CompilerParams(dimension_semantics=semantics, vmem_limit_bytes=limit)


def _nbytes(shape, dtype):
  return math.prod(shape) * jnp.dtype(dtype).itemsize


def _rmsnorm(x, g):
  ms = jnp.mean(x * x, axis=-1, keepdims=True)
  return x * lax.rsqrt(ms + EPS) * g


def _dot(a, b):
  return jnp.dot(a, b, preferred_element_type=F32)


def _dot_nt(a, b):
  return lax.dot_general(a, b, (((1,), (1,)), ((), ())), preferred_element_type=F32)


def _dot_tn(a, b):
  return lax.dot_general(a, b, (((0,), (0,)), ((), ())), preferred_element_type=F32)


def _in_proj_kernel(x_ref, g_ref, w_ref, o_ref):
  n = _rmsnorm(x_ref[...], g_ref[...]).astype(BF16)
  for c in range(D_IN // PROJ_COL_CHUNK):
    cols = slice(c * PROJ_COL_CHUNK, (c + 1) * PROJ_COL_CHUNK)
    o_ref[:, cols] = _dot(n, w_ref[:, cols]).astype(BF16)


def _in_proj(x2d, g, w):
  t = x2d.shape[0]
  blocks = (_nbytes((ROW_TILE, D_MODEL), F32) + _nbytes((D_MODEL, D_IN), BF16)
            + _nbytes((ROW_TILE, D_IN), BF16))
  return pl.pallas_call(
      _in_proj_kernel,
      out_shape=jax.ShapeDtypeStruct((t, D_IN), BF16),
      grid=(t // ROW_TILE,),
      in_specs=[
          pl.BlockSpec((ROW_TILE, D_MODEL), lambda i: (i, 0)),
          pl.BlockSpec((1, D_MODEL), lambda i: (0, 0)),
          pl.BlockSpec((D_MODEL, D_IN), lambda i: (0, 0)),
      ],
      out_specs=pl.BlockSpec((ROW_TILE, D_IN), lambda i: (i, 0)),
      compiler_params=_compiler_params(("parallel",), blocks),
      name="in_proj",
  )(x2d, g, w)


def _rel_bucket(dist):
  max_exact = REL_BUCKETS // 2
  d = jnp.maximum(dist, 0)
  df = jnp.maximum(d, 1).astype(F32)
  large = max_exact + (jnp.log(df / max_exact) / math.log(REL_MAX_DIST / max_exact)
                       * (REL_BUCKETS - max_exact)).astype(jnp.int32)
  large = jnp.minimum(large, REL_BUCKETS - 1)
  return jnp.where(d < max_exact, d, large)


def _bias_kernel(tab_ref, bkt_ref, o_ref):
  h = pl.program_id(0)
  delta = pl.program_id(1)
  bkt = bkt_ref[0]
  w = jnp.zeros(bkt.shape, F32)
  for b in range(REL_BUCKETS):
    w = jnp.where(bkt == b, tab_ref[b, h], w)
  wide = jnp.broadcast_to(w, (MOBA_BLOCK, 2 * MOBA_BLOCK))
  tile = pltpu.roll(wide, MOBA_BLOCK + 1, 1, stride=1, stride_axis=0)[:, :MOBA_BLOCK]
  c = lax.broadcasted_iota(jnp.int32, tile.shape, 0)
  r = lax.broadcasted_iota(jnp.int32, tile.shape, 1)
  o_ref[0, 0] = jnp.where((delta > 0) | (r >= c), tile, NEG_INF)


def _bias_tiles(rel_bias, nb):
  m = jnp.arange(2 * MOBA_BLOCK, dtype=jnp.int32)
  dist = jnp.arange(nb, dtype=jnp.int32)[:, None] * MOBA_BLOCK + m[None, :] - (MOBA_BLOCK - 1)
  bkt = _rel_bucket(dist).reshape(nb, 1, 2 * MOBA_BLOCK)
  return pl.pallas_call(
      _bias_kernel,
      out_shape=jax.ShapeDtypeStruct((MOBA_HEADS, nb, MOBA_BLOCK, MOBA_BLOCK), F32),
      grid=(MOBA_HEADS, nb),
      in_specs=[
          pl.BlockSpec(memory_space=pltpu.SMEM),
          pl.BlockSpec((1, 1, 2 * MOBA_BLOCK), lambda h, d: (d, 0, 0)),
      ],
      out_specs=pl.BlockSpec((1, 1, MOBA_BLOCK, MOBA_BLOCK), lambda h, d: (h, d, 0, 0)),
      compiler_params=_compiler_params(
          ("parallel", "parallel"), _nbytes((MOBA_BLOCK, MOBA_BLOCK), F32)),
      name="bias_tiles",
  )(rel_bias, bkt)


def _moba_kernel(q_ref, k_ref, v_ref, bias_ref, o_ref, kmean_ref, pen_ref, s_ref, *, nb):
  i = pl.program_id(2)
  blk = MOBA_BLOCK
  dh = MOBA_HEAD_DIM

  @pl.when(i == 0)
  def _():
    for j in range(nb):
      kb = k_ref[0, j * blk:(j + 1) * blk, :].astype(F32)
      kmean_ref[j:j + 1, :] = jnp.mean(kb, axis=0, keepdims=True)

  heads = V7X_LANES // dh
  qs = []
  for hh in range(heads):
    lanes = slice(hh * dh, (hh + 1) * dh)
    q = q_ref[0, :, lanes]

    gate = lax.dot_general(kmean_ref[:, lanes], q.astype(F32), (((1,), (1,)), ((), ())),
                           precision=lax.Precision.HIGHEST, preferred_element_type=F32)
    jidx = lax.broadcasted_iota(jnp.int32, gate.shape, 0)
    past = jidx < i
    gate = jnp.where(past, gate, NEG_INF)
    rank = jnp.zeros(gate.shape, F32)
    for jp in range(nb):
      row = gate[jp:jp + 1, :]
      ahead = jnp.where(row > gate, 1.0, jnp.where((row == gate) & (jidx > jp), 1.0, 0.0))
      rank = rank + ahead
    pen_ref[hh] = jnp.where(past, jnp.where(rank < MOBA_TOPK, 0.0, NEG_INF),
                            jnp.where(jidx == i, 0.0, NEG_INF))
    qs.append((q.astype(F32) * (dh ** -0.5)).astype(BF16))

  step = MOBA_STEP_BLOCKS * blk
  trips = (i + MOBA_STEP_BLOCKS) // MOBA_STEP_BLOCKS
  sub = V7X_SUBLANES

  def fold(x):
    return x.reshape(blk // sub, sub, blk)

  def score_pass(t, run_max):
    start = pl.multiple_of(t * step, step)
    out = []
    for hh in range(heads):
      lanes = slice(hh * dh, (hh + 1) * dh)
      s = _dot_nt(k_ref[0, pl.ds(start, step), lanes], qs[hh])
      mx = run_max[hh]
      for u in range(MOBA_STEP_BLOCKS):
        j = t * MOBA_STEP_BLOCKS + u
        part = s[u * blk:(u + 1) * blk] + bias_ref[hh, jnp.maximum(i - j, 0)]
        s_ref[hh, pl.ds(start + u * blk, blk), :] = part
        mx = jnp.maximum(mx, jnp.max(fold(part), axis=0) + pen_ref[hh, pl.ds(j, 1), :])
      out.append(mx)
    return tuple(out)

  run_max = lax.fori_loop(0, trips, score_pass,
                          tuple(jnp.full((sub, blk), NEG_INF, F32) for _ in range(heads)))
  row_max = [jnp.max(mx, axis=0, keepdims=True) for mx in run_max]

  def value_pass(t, carry):
    start = pl.multiple_of(t * step, step)
    out = []
    for hh in range(heads):
      lanes = slice(hh * dh, (hh + 1) * dh)
      lsum, acc = carry[hh]
      probs = []
      for u in range(MOBA_STEP_BLOCKS):
        j = t * MOBA_STEP_BLOCKS + u
        shift = row_max[hh] - pen_ref[hh, pl.ds(j, 1), :]
        p = jnp.exp(s_ref[hh, pl.ds(start + u * blk, blk), :] - shift)
        lsum = lsum + jnp.sum(fold(p), axis=0)
        probs.append(p.astype(BF16))
      acc = acc + _dot_tn(v_ref[0, pl.ds(start, step), lanes], jnp.concatenate(probs, axis=0))
      out.append((lsum, acc))
    return tuple(out)

  final = lax.fori_loop(
      0, trips, value_pass,
      tuple((jnp.zeros((sub, blk), F32), jnp.zeros((dh, blk), F32)) for _ in range(heads)))
  outs = [(acc / jnp.sum(lsum, axis=0, keepdims=True)).T for lsum, acc in final]
  o_ref[0] = jnp.concatenate(outs, axis=1).astype(o_ref.dtype)


def _moba(proj3, bias):
  b, s, _ = proj3.shape
  nb = s // MOBA_BLOCK
  assert nb % MOBA_STEP_BLOCKS == 0, "the inner loop may touch one block past the own block"
  pairs = MOBA_W // V7X_LANES
  per_pair = V7X_LANES // MOBA_HEAD_DIM
  blocks = (2 * _nbytes((MOBA_BLOCK, V7X_LANES), BF16) + 2 * _nbytes((s, V7X_LANES), BF16)
            + _nbytes((per_pair, nb, MOBA_BLOCK, MOBA_BLOCK), F32))
  scratch = (_nbytes((nb, V7X_LANES), F32) + _nbytes((per_pair, nb, MOBA_BLOCK), F32)
             + _nbytes((per_pair, s, MOBA_BLOCK), F32))
  return pl.pallas_call(
      functools.partial(_moba_kernel, nb=nb),
      out_shape=jax.ShapeDtypeStruct((b, s, MOBA_W), BF16),
      grid=(b, pairs, nb),
      in_specs=[
          pl.BlockSpec((1, MOBA_BLOCK, V7X_LANES), lambda bi, hp, i: (bi, i, _QA_BLK + hp)),
          pl.BlockSpec((1, s, V7X_LANES), lambda bi, hp, i: (bi, 0, _KA_BLK + hp)),
          pl.BlockSpec((1, s, V7X_LANES), lambda bi, hp, i: (bi, 0, _VA_BLK + hp)),
          pl.BlockSpec((per_pair, nb, MOBA_BLOCK, MOBA_BLOCK), lambda bi, hp, i: (hp, 0, 0, 0)),
      ],
      out_specs=pl.BlockSpec((1, MOBA_BLOCK, V7X_LANES), lambda bi, hp, i: (bi, i, hp)),
      scratch_shapes=[
          pltpu.VMEM((nb, V7X_LANES), F32),
          pltpu.VMEM((per_pair, nb, MOBA_BLOCK), F32),
          pltpu.VMEM((per_pair, s, MOBA_BLOCK), F32),
      ],
      compiler_params=_compiler_params(("parallel", "parallel", "arbitrary"), blocks, scratch),
      name="moba",
  )(proj3, proj3, proj3, bias)


def _retention_kernel(q_ref, k_ref, v_ref, g_ref, cos_ref, sin_ref, decay_ref, qin_ref, kout_ref,
                      cdec_ref, gain_ref, o_ref, state_ref):
  c_len = RET_CHUNK
  half = RET_DIM // 2

  @pl.when(pl.program_id(2) == 0)
  def _():
    state_ref[...] = jnp.zeros(state_ref.shape, F32)

  for c in range(RET_ROWS // c_len):
    rows = slice(c * c_len, (c + 1) * c_len)
    cos = cos_ref[rows, :]
    sin = sin_ref[rows, :]
    q = q_ref[0, rows, :].astype(F32)
    k = k_ref[0, rows, :].astype(F32)
    v = v_ref[0, rows, :]
    qr = q * cos + pltpu.roll(q, half, 1) * sin
    kr = (k * cos + pltpu.roll(k, half, 1) * sin) * (RET_DIM ** -0.5)
    state = state_ref[...]
    sc = _dot_nt(qr.astype(BF16), kr.astype(BF16)) * decay_ref[0]
    inner = _dot(sc.astype(BF16), v)
    cross = _dot((qr * qin_ref[0]).astype(BF16), state.astype(BF16))
    kv = _dot_tn((kr * kout_ref[0]).astype(BF16), v)
    state_ref[...] = cdec_ref[0] * state + kv
    y = inner + cross
    mu = jnp.mean(y, axis=-1, keepdims=True)
    yc = y - mu
    var = jnp.mean(yc * yc, axis=-1, keepdims=True)
    yn = yc * lax.rsqrt(var + EPS) * gain_ref[...]
    g = g_ref[0, rows, :].astype(F32)
    o_ref[0, rows, :] = (yn * (g * jax.nn.sigmoid(g))).astype(o_ref.dtype)


def _retention(proj3, gn_gain):
  b, s, _ = proj3.shape
  c_len = RET_CHUNK
  half = RET_DIM // 2
  pos = jnp.arange(s, dtype=F32)
  inv = ROPE_BASE ** (-jnp.arange(half, dtype=F32) / half)
  ang = pos[:, None] * inv
  cos = jnp.concatenate([jnp.cos(ang), jnp.cos(ang)], axis=-1)
  sin = jnp.concatenate([-jnp.sin(ang), jnp.sin(ang)], axis=-1)
  log_gamma = jnp.log1p(-jnp.power(2.0, -5.0 - jnp.arange(RET_HEADS, dtype=F32)))
  idx = jnp.arange(c_len, dtype=F32)
  diff = idx[:, None] - idx[None, :]
  decay = jnp.where(diff >= 0, jnp.exp(log_gamma[:, None, None] * jnp.maximum(diff, 0.0)), 0.0)
  lanes = (RET_HEADS, c_len, RET_DIM)
  q_in = jnp.broadcast_to(jnp.exp(log_gamma[:, None] * (idx + 1.0))[:, :, None], lanes)
  k_out = jnp.broadcast_to(jnp.exp(log_gamma[:, None] * (c_len - 1.0 - idx))[:, :, None], lanes)
  cdec = jnp.broadcast_to(jnp.exp(log_gamma * c_len)[:, None, None], (RET_HEADS, 1, RET_DIM))

  row_blk = lambda off: pl.BlockSpec((1, RET_ROWS, RET_DIM), lambda bi, h, t: (bi, t, off + h))
  head_blk = lambda shape: pl.BlockSpec((1,) + shape, lambda bi, h, t: (h, 0, 0))
  blocks = (5 * _nbytes((RET_ROWS, RET_DIM), BF16) + 2 * _nbytes((RET_ROWS, RET_DIM), F32)
            + 3 * _nbytes((c_len, RET_DIM), F32))
  return pl.pallas_call(
      _retention_kernel,
      out_shape=jax.ShapeDtypeStruct((b, s, RET_W), BF16),
      grid=(b, RET_HEADS, s // RET_ROWS),
      in_specs=[
          row_blk(_QR_BLK), row_blk(_KR_BLK), row_blk(_VR_BLK), row_blk(_GR_BLK),
          pl.BlockSpec((RET_ROWS, RET_DIM), lambda bi, h, t: (t, 0)),
          pl.BlockSpec((RET_ROWS, RET_DIM), lambda bi, h, t: (t, 0)),
          head_blk((c_len, c_len)), head_blk((c_len, RET_DIM)), head_blk((c_len, RET_DIM)),
          head_blk((1, RET_DIM)),
          pl.BlockSpec((1, RET_DIM), lambda bi, h, t: (0, h)),
      ],
      out_specs=pl.BlockSpec((1, RET_ROWS, RET_DIM), lambda bi, h, t: (bi, t, h)),
      scratch_shapes=[pltpu.VMEM((RET_DIM, RET_DIM), F32)],
      compiler_params=_compiler_params(("parallel", "parallel", "arbitrary"), blocks,
                                       _nbytes((RET_DIM, RET_DIM), F32)),
      name="retention",
  )(proj3, proj3, proj3, proj3, cos, sin, decay, q_in, k_out, cdec, gn_gain)


def _mem_kv_kernel(m_ref, g_ref, w_ref, o_ref):
  n = _rmsnorm(m_ref[0], g_ref[...]).astype(BF16)
  o_ref[0] = _dot(n, w_ref[...]).astype(BF16)


def _mem_kv(mem, g, w):
  b = mem.shape[0]
  blocks = (_nbytes((MEM_LEN, D_MODEL), F32) + _nbytes((D_MODEL, 2 * MEM_W), BF16)
            + _nbytes((MEM_LEN, 2 * MEM_W), BF16))
  return pl.pallas_call(
      _mem_kv_kernel,
      out_shape=jax.ShapeDtypeStruct((b, MEM_LEN, 2 * MEM_W), BF16),
      grid=(b,),
      in_specs=[
          pl.BlockSpec((1, MEM_LEN, D_MODEL), lambda i: (i, 0, 0)),
          pl.BlockSpec((1, D_MODEL), lambda i: (0, 0)),
          pl.BlockSpec((D_MODEL, 2 * MEM_W), lambda i: (0, 0)),
      ],
      out_specs=pl.BlockSpec((1, MEM_LEN, 2 * MEM_W), lambda i: (i, 0, 0)),
      compiler_params=_compiler_params(("parallel",), blocks),
      name="mem_kv",
  )(mem, g, w)


def _mem_attn_kernel(q_ref, kv_ref, o_ref):
  dh = MEM_HEAD_DIM
  outs = []
  for h in range(MEM_HEADS):
    q = q_ref[0, :, h * dh:(h + 1) * dh]
    k = kv_ref[0, :, h * dh:(h + 1) * dh]
    v = kv_ref[0, :, MEM_W + h * dh:MEM_W + (h + 1) * dh]
    s = _dot_nt(q, k) * (dh ** -0.5)
    p = jnp.exp(s - jnp.max(s, axis=-1, keepdims=True))
    l = jnp.sum(p, axis=-1, keepdims=True)
    outs.append(_dot(p.astype(BF16), v) / l)
  o_ref[0] = jnp.concatenate(outs, axis=1).astype(o_ref.dtype)


def _mem_attn(proj3, kv):
  b, s, _ = proj3.shape
  blocks = (2 * _nbytes((ROW_TILE, MEM_W), BF16) + _nbytes((MEM_LEN, 2 * MEM_W), BF16))
  return pl.pallas_call(
      _mem_attn_kernel,
      out_shape=jax.ShapeDtypeStruct((b, s, MEM_W), BF16),
      grid=(b, s // ROW_TILE),
      in_specs=[
          pl.BlockSpec((1, ROW_TILE, MEM_W), lambda bi, t: (bi, t, _QM_COL // MEM_W)),
          pl.BlockSpec((1, MEM_LEN, 2 * MEM_W), lambda bi, t: (bi, 0, 0)),
      ],
      out_specs=pl.BlockSpec((1, ROW_TILE, MEM_W), lambda bi, t: (bi, t, 0)),
      compiler_params=_compiler_params(("parallel", "parallel"), blocks),
      name="mem_attn",
  )(proj3, kv)


def _merge_kernel(x_ref, ya_ref, yr_ref, ym_ref, za_ref, zr_ref, zm_ref, wa_ref, wr_ref, wm_ref,
                  wo_ref, o_ref):
  def branch(y_ref, z_ref, w_ref):
    return jax.nn.sigmoid(z_ref[...].astype(F32)) * _dot(y_ref[...], w_ref[...])

  merged = (branch(ya_ref, za_ref, wa_ref) + branch(yr_ref, zr_ref, wr_ref)
            + branch(ym_ref, zm_ref, wm_ref))
  o_ref[...] = x_ref[...] + _dot(merged.astype(BF16), wo_ref[...])


def _merge(x2d, ya, yr, ym, proj2d, wa, wr, wm, wo):
  t = x2d.shape[0]
  zblk = _Z_COL // D_MODEL
  row = lambda w, col=0: pl.BlockSpec((ROW_TILE, w), lambda i: (i, col))
  full = lambda shape: pl.BlockSpec(shape, lambda i: (0, 0))
  blocks = (2 * _nbytes((ROW_TILE, D_MODEL), F32) + 3 * _nbytes((ROW_TILE, MOBA_W), BF16)
            + 3 * _nbytes((ROW_TILE, D_MODEL), BF16) + 3 * _nbytes((MOBA_W, D_MODEL), BF16)
            + _nbytes((D_MODEL, D_MODEL), BF16))
  return pl.pallas_call(
      _merge_kernel,
      out_shape=jax.ShapeDtypeStruct((t, D_MODEL), F32),
      grid=(t // ROW_TILE,),
      in_specs=[
          row(D_MODEL), row(MOBA_W), row(RET_W), row(MEM_W),
          row(D_MODEL, zblk), row(D_MODEL, zblk + 1), row(D_MODEL, zblk + 2),
          full((MOBA_W, D_MODEL)), full((RET_W, D_MODEL)), full((MEM_W, D_MODEL)),
          full((D_MODEL, D_MODEL)),
      ],
      out_specs=row(D_MODEL),
      compiler_params=_compiler_params(("parallel",), blocks),
      name="merge",
  )(x2d, ya, yr, ym, proj2d, proj2d, proj2d, wa, wr, wm, wo)


def _ffn_kernel(h_ref, gf_ref, wup_ref, cw_ref, cb_ref, wdn_ref, gfin_ref, o_ref, buf_ref,
                carry_ref):
  tm = ROW_TILE
  halo = CONV_HALO
  cw = FFN_COL_CHUNK
  t = pl.program_id(1)

  @pl.when(t == 0)
  def _():
    carry_ref[...] = jnp.zeros(carry_ref.shape, F32)

  h1 = h_ref[0]
  n = _rmsnorm(h1, gf_ref[...]).astype(BF16)
  acc = jnp.zeros((tm, D_MODEL), F32)
  for c in range(D_FF // cw):
    halves = []
    for part in range(2):
      col = part * D_FF + c * cw
      cols = slice(col, col + cw)
      slot = 2 * c + part
      buf_ref[0:halo, :] = carry_ref[slot]
      buf_ref[halo:halo + tm, :] = _dot(n, wup_ref[:, cols])
      carry_ref[slot] = buf_ref[tm:tm + halo, :]
      conv = cb_ref[:, cols]
      for tap in range(CONV_WIDTH):
        shift = CONV_WIDTH - 1 - tap
        conv = conv + cw_ref[tap:tap + 1, cols] * buf_ref[halo - shift:halo - shift + tm, :]
      halves.append(conv)
    gate, up = halves
    act = 0.5 * gate * (1.0 + lax.erf(gate * math.sqrt(0.5))) * up
    acc = acc + _dot(act.astype(BF16), wdn_ref[c * cw:(c + 1) * cw, :])
  o_ref[0] = _rmsnorm(h1 + acc, gfin_ref[...])


def _ffn(h3, g_ffn, w_up, conv_w, conv_b, w_down, g_final):
  b, s, _ = h3.shape
  full = lambda shape: pl.BlockSpec(shape, lambda bi, t: (0, 0))
  blocks = (2 * _nbytes((ROW_TILE, D_MODEL), F32) + _nbytes((D_MODEL, 2 * D_FF), BF16)
            + _nbytes((D_FF, D_MODEL), BF16) + _nbytes((CONV_WIDTH + 1, 2 * D_FF), F32))
  slots = 2 * D_FF // FFN_COL_CHUNK
  scratch = (_nbytes((ROW_TILE + CONV_HALO, FFN_COL_CHUNK), F32)
             + _nbytes((slots, CONV_HALO, FFN_COL_CHUNK), F32))
  return pl.pallas_call(
      _ffn_kernel,
      out_shape=jax.ShapeDtypeStruct((b, s, D_MODEL), F32),
      grid=(b, s // ROW_TILE),
      in_specs=[
          pl.BlockSpec((1, ROW_TILE, D_MODEL), lambda bi, t: (bi, t, 0)),
          full((1, D_MODEL)),
          full((D_MODEL, 2 * D_FF)),
          full((CONV_WIDTH, 2 * D_FF)),
          full((1, 2 * D_FF)),
          full((D_FF, D_MODEL)),
          full((1, D_MODEL)),
      ],
      out_specs=pl.BlockSpec((1, ROW_TILE, D_MODEL), lambda bi, t: (bi, t, 0)),
      scratch_shapes=[
          pltpu.VMEM((ROW_TILE + CONV_HALO, FFN_COL_CHUNK), F32),
          pltpu.VMEM((slots, CONV_HALO, FFN_COL_CHUNK), F32),
      ],
      compiler_params=_compiler_params(("parallel", "arbitrary"), blocks, scratch),
      name="ffn",
  )(h3, g_ffn, w_up, conv_w, conv_b, w_down, g_final)


def kernel(x, mem, g_mix, w_in, rel_bias, ret_gn_gain, g_mem, w_mem_kv, w_br_attn, w_br_ret,
           w_br_mem, w_out, g_ffn, w_up, conv_w, conv_b, w_down, g_final):
  b, s, d = x.shape
  assert d == D_MODEL and mem.shape == (b, MEM_LEN, D_MODEL)
  assert s % MOBA_BLOCK == 0 and s % ROW_TILE == 0 and s % RET_ROWS == 0
  assert g_mix.shape[0] == 1, "single-layer block"
  h = x.reshape(b * s, d)
  bias = _bias_tiles(rel_bias, s // MOBA_BLOCK)
  proj = _in_proj(h, g_mix, w_in[0].astype(BF16))
  proj3 = proj.reshape(b, s, D_IN)
  y_a = _moba(proj3, bias)
  y_r = _retention(proj3, ret_gn_gain)
  kv = _mem_kv(mem, g_mem, w_mem_kv[0].astype(BF16))
  y_m = _mem_attn(proj3, kv)
  h1 = _merge(h, y_a.reshape(b * s, MOBA_W), y_r.reshape(b * s, RET_W), y_m.reshape(b * s, MEM_W),
              proj, w_br_attn[0].astype(BF16), w_br_ret[0].astype(BF16),
              w_br_mem[0].astype(BF16), w_out[0].astype(BF16))
  return _ffn(h1.reshape(b, s, d), g_ffn, w_up[0].astype(BF16), conv_w[0], conv_b,
              w_down[0].astype(BF16), g_final.reshape(1, d))
```

```python
import functools
import math

import jax
import jax.numpy as jnp
from jax import lax
from jax.experimental import pallas as pl
from jax.experimental.pallas import tpu as pltpu

F32 = jnp.float32
BF16 = jnp.bfloat16

D_MODEL = 1024
MEM_LEN = 256
MOBA_HEADS = 8
MOBA_HEAD_DIM = 64
MOBA_BLOCK = 256
MOBA_TOPK = 3
RET_HEADS = 4
RET_DIM = 128
RET_CHUNK = 128
ROPE_BASE = 10000.0
MEM_HEADS = 4
MEM_HEAD_DIM = 128
REL_BUCKETS = 32
REL_MAX_DIST = 2048
D_FF = 2816
CONV_WIDTH = 3
EPS = 1e-6
NEG_INF = -1e30

MOBA_W = MOBA_HEADS * MOBA_HEAD_DIM
RET_W = RET_HEADS * RET_DIM
MEM_W = MEM_HEADS * MEM_HEAD_DIM
D_IN = 3 * MOBA_W + 4 * RET_W + MEM_W + 3 * D_MODEL

V7X_LANES = 128
V7X_SUBLANES = 8
V7X_VMEM_BYTES = 64 * 1024 * 1024
V7X_VMEM_RESERVE_BYTES = 6 * 1024 * 1024

_QA_BLK = 0
_KA_BLK = MOBA_W // V7X_LANES
_VA_BLK = 2 * MOBA_W // V7X_LANES
_QR_BLK = 3 * MOBA_W // V7X_LANES
_KR_BLK = _QR_BLK + RET_W // V7X_LANES
_VR_BLK = _KR_BLK + RET_W // V7X_LANES
_GR_BLK = _VR_BLK + RET_W // V7X_LANES
_QM_COL = 3 * MOBA_W + 4 * RET_W
_Z_COL = _QM_COL + MEM_W

ROW_TILE = 512
PROJ_COL_CHUNK = 512
RET_ROWS = 512
FFN_COL_CHUNK = 256
MOBA_STEP_BLOCKS = 4
CONV_HALO = V7X_SUBLANES


def _compiler_params(semantics, block_bytes, scratch_bytes=0):
  need = 2 * block_bytes + scratch_bytes + 12 * 1024 * 1024
  limit = min(need, V7X_VMEM_BYTES - V7X_VMEM_RESERVE_BYTES)
  return pltpu.CompilerParams(dimension_semantics=semantics, vmem_limit_bytes=limit)


def _nbytes(shape, dtype):
  return math.prod(shape) * jnp.dtype(dtype).itemsize


def _rmsnorm(x, g):
  ms = jnp.mean(x * x, axis=-1, keepdims=True)
  return x * lax.rsqrt(ms + EPS) * g


def _dot(a, b):
  return jnp.dot(a, b, preferred_element_type=F32)


def _dot_nt(a, b):
  return lax.dot_general(a, b, (((1,), (1,)), ((), ())), preferred_element_type=F32)


def _dot_tn(a, b):
  return lax.dot_general(a, b, (((0,), (0,)), ((), ())), preferred_element_type=F32)


def _in_proj_kernel(x_ref, g_ref, w_ref, o_ref):
  n = _rmsnorm(x_ref[...], g_ref[...]).astype(BF16)
  for c in range(D_IN // PROJ_COL_CHUNK):
    cols = slice(c * PROJ_COL_CHUNK, (c + 1) * PROJ_COL_CHUNK)
    o_ref[:, cols] = _dot(n, w_ref[:, cols]).astype(BF16)


def _in_proj(x2d, g, w):
  t = x2d.shape[0]
  blocks = (_nbytes((ROW_TILE, D_MODEL), F32) + _nbytes((D_MODEL, D_IN), BF16)
            + _nbytes((ROW_TILE, D_IN), BF16))
  return pl.pallas_call(
      _in_proj_kernel,
      out_shape=jax.ShapeDtypeStruct((t, D_IN), BF16),
      grid=(t // ROW_TILE,),
      in_specs=[
          pl.BlockSpec((ROW_TILE, D_MODEL), lambda i: (i, 0)),
          pl.BlockSpec((1, D_MODEL), lambda i: (0, 0)),
          pl.BlockSpec((D_MODEL, D_IN), lambda i: (0, 0)),
      ],
      out_specs=pl.BlockSpec((ROW_TILE, D_IN), lambda i: (i, 0)),
      compiler_params=_compiler_params(("parallel",), blocks),
      name="in_proj",
  )(x2d, g, w)


def _rel_bucket(dist):
  max_exact = REL_BUCKETS // 2
  d = jnp.maximum(dist, 0)
  df = jnp.maximum(d, 1).astype(F32)
  large = max_exact + (jnp.log(df / max_exact) / math.log(REL_MAX_DIST / max_exact)
                       * (REL_BUCKETS - max_exact)).astype(jnp.int32)
  large = jnp.minimum(large, REL_BUCKETS - 1)
  return jnp.where(d < max_exact, d, large)


def _bias_kernel(tab_ref, bkt_ref, o_ref):
  h = pl.program_id(0)
  delta = pl.program_id(1)
  bkt = bkt_ref[0]
  w = jnp.zeros(bkt.shape, F32)
  for b in range(REL_BUCKETS):
    w = jnp.where(bkt == b, tab_ref[b, h], w)
  wide = jnp.broadcast_to(w, (MOBA_BLOCK, 2 * MOBA_BLOCK))
  tile = pltpu.roll(wide, MOBA_BLOCK + 1, 1, stride=1, stride_axis=0)[:, :MOBA_BLOCK]
  c = lax.broadcasted_iota(jnp.int32, tile.shape, 0)
  r = lax.broadcasted_iota(jnp.int32, tile.shape, 1)
  o_ref[0, 0] = jnp.where((delta > 0) | (r >= c), tile, NEG_INF)


def _bias_tiles(rel_bias, nb):
  m = jnp.arange(2 * MOBA_BLOCK, dtype=jnp.int32)
  dist = jnp.arange(nb, dtype=jnp.int32)[:, None] * MOBA_BLOCK + m[None, :] - (MOBA_BLOCK - 1)
  bkt = _rel_bucket(dist).reshape(nb, 1, 2 * MOBA_BLOCK)
  return pl.pallas_call(
      _bias_kernel,
      out_shape=jax.ShapeDtypeStruct((MOBA_HEADS, nb, MOBA_BLOCK, MOBA_BLOCK), F32),
      grid=(MOBA_HEADS, nb),
      in_specs=[
          pl.BlockSpec(memory_space=pltpu.SMEM),
          pl.BlockSpec((1, 1, 2 * MOBA_BLOCK), lambda h, d: (d, 0, 0)),
      ],
      out_specs=pl.BlockSpec((1, 1, MOBA_BLOCK, MOBA_BLOCK), lambda h, d: (h, d, 0, 0)),
      compiler_params=_compiler_params(
          ("parallel", "parallel"), _nbytes((MOBA_BLOCK, MOBA_BLOCK), F32)),
      name="bias_tiles",
  )(rel_bias, bkt)


def _moba_kernel(q_ref, qall_ref, k_ref, v_ref, bias_ref, o_ref, pen_ref, s_ref, *, nb):
  i = pl.program_id(2)
  blk = MOBA_BLOCK
  dh = MOBA_HEAD_DIM
  heads = V7X_LANES // dh
  seq = nb * blk
  sub = V7X_SUBLANES

  def head_lanes(hh, shape):
    lane = lax.broadcasted_iota(jnp.int32, shape, len(shape) - 1)
    return (lane >= hh * dh) & (lane < (hh + 1) * dh)

  @pl.when(i == 0)
  def _():
    kmean = jnp.concatenate(
        [jnp.mean(k_ref[0, j * blk:(j + 1) * blk, :].astype(F32), axis=0, keepdims=True)
         for j in range(nb)], axis=0)
    q_all = qall_ref[0].astype(F32)
    jidx = lax.broadcasted_iota(jnp.int32, (nb, seq), 0)
    qblk = lax.broadcasted_iota(jnp.int32, (nb, seq), 1) // blk
    past = jidx < qblk
    for hh in range(heads):
      km = jnp.where(head_lanes(hh, kmean.shape), kmean, 0.0)
      gate = lax.dot_general(km, q_all, (((1,), (1,)), ((), ())),
                             precision=lax.Precision.HIGHEST, preferred_element_type=F32)
      gate = jnp.where(past, gate, NEG_INF)
      rank = jnp.zeros(gate.shape, F32)
      for jp in range(nb):
        row = gate[jp:jp + 1, :]
        ahead = jnp.where(row > gate, 1.0, jnp.where((row == gate) & (jidx > jp), 1.0, 0.0))
        rank = rank + ahead
      pen_ref[hh] = jnp.where(past, jnp.where(rank < MOBA_TOPK, 0.0, NEG_INF),
                              jnp.where(jidx == qblk, 0.0, NEG_INF))

  q = q_ref[0].astype(F32) * (dh ** -0.5)
  qz = [jnp.where(head_lanes(hh, q.shape), q, 0.0).astype(BF16) for hh in range(heads)]
  qcols = pl.ds(pl.multiple_of(i * blk, blk), blk)

  step = MOBA_STEP_BLOCKS * blk
  trips = (i + MOBA_STEP_BLOCKS) // MOBA_STEP_BLOCKS

  def fold(x):
    return x.reshape(blk // sub, sub, blk)

  def score_pass(t, run_max):
    start = pl.multiple_of(t * step, step)
    k = k_ref[0, pl.ds(start, step), :]
    out = []
    for hh in range(heads):
      s = _dot_nt(k, qz[hh])
      mx = run_max[hh]
      for u in range(MOBA_STEP_BLOCKS):
        j = t * MOBA_STEP_BLOCKS + u
        part = s[u * blk:(u + 1) * blk] + bias_ref[hh, jnp.maximum(i - j, 0)]
        s_ref[hh, pl.ds(start + u * blk, blk), :] = part
        mx = jnp.maximum(mx, jnp.max(fold(part), axis=0) + pen_ref[hh, pl.ds(j, 1), qcols])
      out.append(mx)
    return tuple(out)

  run_max = lax.fori_loop(0, trips, score_pass,
                          tuple(jnp.full((sub, blk), NEG_INF, F32) for _ in range(heads)))
  row_max = [jnp.max(mx, axis=0, keepdims=True) for mx in run_max]

  def value_pass(t, carry):
    start = pl.multiple_of(t * step, step)
    v = v_ref[0, pl.ds(start, step), :]
    out = []
    for hh in range(heads):
      lsum, acc = carry[hh]
      probs = []
      for u in range(MOBA_STEP_BLOCKS):
        j = t * MOBA_STEP_BLOCKS + u
        shift = row_max[hh] - pen_ref[hh, pl.ds(j, 1), qcols]
        p = jnp.exp(s_ref[hh, pl.ds(start + u * blk, blk), :] - shift)
        lsum = lsum + jnp.sum(fold(p), axis=0)
        probs.append(p.astype(BF16))
      pv = _dot_tn(v, jnp.concatenate(probs, axis=0))
      out.append((lsum, acc + pv[hh * dh:(hh + 1) * dh]))
    return tuple(out)

  final = lax.fori_loop(
      0, trips, value_pass,
      tuple((jnp.zeros((sub, blk), F32), jnp.zeros((dh, blk), F32)) for _ in range(heads)))
  for hh, (lsum, acc) in enumerate(final):
    out = acc / jnp.sum(lsum, axis=0, keepdims=True)
    o_ref[0, hh * dh:(hh + 1) * dh, :] = out.astype(o_ref.dtype)


def _moba(proj3, bias):
  b, s, _ = proj3.shape
  nb = s // MOBA_BLOCK
  assert nb % MOBA_STEP_BLOCKS == 0, "the inner loops may touch blocks past the own block"
  pairs = MOBA_W // V7X_LANES
  per_pair = V7X_LANES // MOBA_HEAD_DIM
  blocks = (2 * _nbytes((MOBA_BLOCK, V7X_LANES), BF16) + 3 * _nbytes((s, V7X_LANES), BF16)
            + _nbytes((per_pair, nb, MOBA_BLOCK, MOBA_BLOCK), F32))
  scratch = _nbytes((per_pair, nb, s), F32) + _nbytes((per_pair, s, MOBA_BLOCK), F32)
  seq_blk = lambda off: pl.BlockSpec((1, s, V7X_LANES), lambda bi, hp, i: (bi, 0, off + hp))
  return pl.pallas_call(
      functools.partial(_moba_kernel, nb=nb),
      out_shape=jax.ShapeDtypeStruct((b, MOBA_W, s), BF16),
      grid=(b, pairs, nb),
      in_specs=[
          pl.BlockSpec((1, MOBA_BLOCK, V7X_LANES), lambda bi, hp, i: (bi, i, _QA_BLK + hp)),
          seq_blk(_QA_BLK), seq_blk(_KA_BLK), seq_blk(_VA_BLK),
          pl.BlockSpec((per_pair, nb, MOBA_BLOCK, MOBA_BLOCK), lambda bi, hp, i: (hp, 0, 0, 0)),
      ],
      out_specs=pl.BlockSpec((1, V7X_LANES, MOBA_BLOCK), lambda bi, hp, i: (bi, hp, i)),
      scratch_shapes=[
          pltpu.VMEM((per_pair, nb, s), F32),
          pltpu.VMEM((per_pair, s, MOBA_BLOCK), F32),
      ],
      compiler_params=_compiler_params(("parallel", "parallel", "arbitrary"), blocks, scratch),
      name="moba",
  )(proj3, proj3, proj3, proj3, bias)


def _retention_kernel(q_ref, k_ref, v_ref, g_ref, cos_ref, sin_ref, decay_ref, qin_ref, kout_ref,
                      cdec_ref, gain_ref, o_ref, state_ref):
  c_len = RET_CHUNK
  half = RET_DIM // 2

  @pl.when(pl.program_id(2) == 0)
  def _():
    state_ref[...] = jnp.zeros(state_ref.shape, F32)

  for c in range(RET_ROWS // c_len):
    rows = slice(c * c_len, (c + 1) * c_len)
    cos = cos_ref[rows, :]
    sin = sin_ref[rows, :]
    q = q_ref[0, rows, :].astype(F32)
    k = k_ref[0, rows, :].astype(F32)
    v = v_ref[0, rows, :]
    qr = q * cos + pltpu.roll(q, half, 1) * sin
    kr = (k * cos + pltpu.roll(k, half, 1) * sin) * (RET_DIM ** -0.5)
    state = state_ref[...]
    sc = _dot_nt(qr.astype(BF16), kr.astype(BF16)) * decay_ref[0]
    inner = _dot(sc.astype(BF16), v)
    cross = _dot((qr * qin_ref[0]).astype(BF16), state.astype(BF16))
    kv = _dot_tn((kr * kout_ref[0]).astype(BF16), v)
    state_ref[...] = cdec_ref[0] * state + kv
    y = inner + cross
    mu = jnp.mean(y, axis=-1, keepdims=True)
    yc = y - mu
    var = jnp.mean(yc * yc, axis=-1, keepdims=True)
    yn = yc * lax.rsqrt(var + EPS) * gain_ref[...]
    g = g_ref[0, rows, :].astype(F32)
    o_ref[0, rows, :] = (yn * (g * jax.nn.sigmoid(g))).astype(o_ref.dtype)


def _retention(proj3, gn_gain):
  b, s, _ = proj3.shape
  c_len = RET_CHUNK
  half = RET_DIM // 2
  pos = jnp.arange(s, dtype=F32)
  inv = ROPE_BASE ** (-jnp.arange(half, dtype=F32) / half)
  ang = pos[:, None] * inv
  cos = jnp.concatenate([jnp.cos(ang), jnp.cos(ang)], axis=-1)
  sin = jnp.concatenate([-jnp.sin(ang), jnp.sin(ang)], axis=-1)
  log_gamma = jnp.log1p(-jnp.power(2.0, -5.0 - jnp.arange(RET_HEADS, dtype=F32)))
  idx = jnp.arange(c_len, dtype=F32)
  diff = idx[:, None] - idx[None, :]
  decay = jnp.where(diff >= 0, jnp.exp(log_gamma[:, None, None] * jnp.maximum(diff, 0.0)), 0.0)
  lanes = (RET_HEADS, c_len, RET_DIM)
  q_in = jnp.broadcast_to(jnp.exp(log_gamma[:, None] * (idx + 1.0))[:, :, None], lanes)
  k_out = jnp.broadcast_to(jnp.exp(log_gamma[:, None] * (c_len - 1.0 - idx))[:, :, None], lanes)
  cdec = jnp.broadcast_to(jnp.exp(log_gamma * c_len)[:, None, None], (RET_HEADS, 1, RET_DIM))

  row_blk = lambda off: pl.BlockSpec((1, RET_ROWS, RET_DIM), lambda bi, h, t: (bi, t, off + h))
  head_blk = lambda shape: pl.BlockSpec((1,) + shape, lambda bi, h, t: (h, 0, 0))
  blocks = (5 * _nbytes((RET_ROWS, RET_DIM), BF16) + 2 * _nbytes((RET_ROWS, RET_DIM), F32)
            + 3 * _nbytes((c_len, RET_DIM), F32))
  return pl.pallas_call(
      _retention_kernel,
      out_shape=jax.ShapeDtypeStruct((b, s, RET_W), BF16),
      grid=(b, RET_HEADS, s // RET_ROWS),
      in_specs=[
          row_blk(_QR_BLK), row_blk(_KR_BLK), row_blk(_VR_BLK), row_blk(_GR_BLK),
          pl.BlockSpec((RET_ROWS, RET_DIM), lambda bi, h, t: (t, 0)),
          pl.BlockSpec((RET_ROWS, RET_DIM), lambda bi, h, t: (t, 0)),
          head_blk((c_len, c_len)), head_blk((c_len, RET_DIM)), head_blk((c_len, RET_DIM)),
          head_blk((1, RET_DIM)),
          pl.BlockSpec((1, RET_DIM), lambda bi, h, t: (0, h)),
      ],
      out_specs=pl.BlockSpec((1, RET_ROWS, RET_DIM), lambda bi, h, t: (bi, t, h)),
      scratch_shapes=[pltpu.VMEM((RET_DIM, RET_DIM), F32)],
      compiler_params=_compiler_params(("parallel", "parallel", "arbitrary"), blocks,
                                       _nbytes((RET_DIM, RET_DIM), F32)),
      name="retention",
  )(proj3, proj3, proj3, proj3, cos, sin, decay, q_in, k_out, cdec, gn_gain)


def _mem_kv_kernel(m_ref, g_ref, w_ref, o_ref):
  n = _rmsnorm(m_ref[0], g_ref[...]).astype(BF16)
  o_ref[0] = _dot(n, w_ref[...]).astype(BF16)


def _mem_kv(mem, g, w):
  b = mem.shape[0]
  blocks = (_nbytes((MEM_LEN, D_MODEL), F32) + _nbytes((D_MODEL, 2 * MEM_W), BF16)
            + _nbytes((MEM_LEN, 2 * MEM_W), BF16))
  return pl.pallas_call(
      _mem_kv_kernel,
      out_shape=jax.ShapeDtypeStruct((b, MEM_LEN, 2 * MEM_W), BF16),
      grid=(b,),
      in_specs=[
          pl.BlockSpec((1, MEM_LEN, D_MODEL), lambda i: (i, 0, 0)),
          pl.BlockSpec((1, D_MODEL), lambda i: (0, 0)),
          pl.BlockSpec((D_MODEL, 2 * MEM_W), lambda i: (0, 0)),
      ],
      out_specs=pl.BlockSpec((1, MEM_LEN, 2 * MEM_W), lambda i: (i, 0, 0)),
      compiler_params=_compiler_params(("parallel",), blocks),
      name="mem_kv",
  )(mem, g, w)


def _mem_attn_kernel(q_ref, kv_ref, o_ref):
  dh = MEM_HEAD_DIM
  outs = []
  for h in range(MEM_HEADS):
    q = q_ref[0, :, h * dh:(h + 1) * dh]
    k = kv_ref[0, :, h * dh:(h + 1) * dh]
    v = kv_ref[0, :, MEM_W + h * dh:MEM_W + (h + 1) * dh]
    s = _dot_nt(q, k) * (dh ** -0.5)
    p = jnp.exp(s - jnp.max(s, axis=-1, keepdims=True))
    l = jnp.sum(p, axis=-1, keepdims=True)
    outs.append(_dot(p.astype(BF16), v) / l)
  o_ref[0] = jnp.concatenate(outs, axis=1).astype(o_ref.dtype)


def _mem_attn(proj3, kv):
  b, s, _ = proj3.shape
  blocks = (2 * _nbytes((ROW_TILE, MEM_W), BF16) + _nbytes((MEM_LEN, 2 * MEM_W), BF16))
  return pl.pallas_call(
      _mem_attn_kernel,
      out_shape=jax.ShapeDtypeStruct((b, s, MEM_W), BF16),
      grid=(b, s // ROW_TILE),
      in_specs=[
          pl.BlockSpec((1, ROW_TILE, MEM_W), lambda bi, t: (bi, t, _QM_COL // MEM_W)),
          pl.BlockSpec((1, MEM_LEN, 2 * MEM_W), lambda bi, t: (bi, 0, 0)),
      ],
      out_specs=pl.BlockSpec((1, ROW_TILE, MEM_W), lambda bi, t: (bi, t, 0)),
      compiler_params=_compiler_params(("parallel", "parallel"), blocks),
      name="mem_attn",
  )(proj3, kv)


def _merge_kernel(x_ref, ya_ref, yr_ref, ym_ref, za_ref, zr_ref, zm_ref, wa_ref, wr_ref, wm_ref,
                  wo_ref, o_ref):
  def gate(z_ref, proj):
    return jax.nn.sigmoid(z_ref[...].astype(F32)) * proj

  merged = (gate(za_ref, _dot_tn(ya_ref[0], wa_ref[...]))
            + gate(zr_ref, _dot(yr_ref[...], wr_ref[...]))
            + gate(zm_ref, _dot(ym_ref[...], wm_ref[...])))
  o_ref[...] = x_ref[...] + _dot(merged.astype(BF16), wo_ref[...])


def _merge(x2d, ya_t, yr, ym, proj2d, wa, wr, wm, wo):
  t = x2d.shape[0]
  tiles_per_seq = ya_t.shape[2] // ROW_TILE
  zblk = _Z_COL // D_MODEL
  row = lambda w, col=0: pl.BlockSpec((ROW_TILE, w), lambda i: (i, col))
  ya_spec = pl.BlockSpec((1, MOBA_W, ROW_TILE),
                         lambda i: (i // tiles_per_seq, 0, i % tiles_per_seq))
  full = lambda shape: pl.BlockSpec(shape, lambda i: (0, 0))
  blocks = (2 * _nbytes((ROW_TILE, D_MODEL), F32) + 3 * _nbytes((ROW_TILE, MOBA_W), BF16)
            + 3 * _nbytes((ROW_TILE, D_MODEL), BF16) + 3 * _nbytes((MOBA_W, D_MODEL), BF16)
            + _nbytes((D_MODEL, D_MODEL), BF16))
  return pl.pallas_call(
      _merge_kernel,
      out_shape=jax.ShapeDtypeStruct((t, D_MODEL), F32),
      grid=(t // ROW_TILE,),
      in_specs=[
          row(D_MODEL), ya_spec, row(RET_W), row(MEM_W),
          row(D_MODEL, zblk), row(D_MODEL, zblk + 1), row(D_MODEL, zblk + 2),
          full((MOBA_W, D_MODEL)), full((RET_W, D_MODEL)), full((MEM_W, D_MODEL)),
          full((D_MODEL, D_MODEL)),
      ],
      out_specs=row(D_MODEL),
      compiler_params=_compiler_params(("parallel",), blocks),
      name="merge",
  )(x2d, ya_t, yr, ym, proj2d, proj2d, proj2d, wa, wr, wm, wo)


def _ffn_kernel(h_ref, gf_ref, wup_ref, cw_ref, cb_ref, wdn_ref, gfin_ref, o_ref, buf_ref,
                carry_ref):
  tm = ROW_TILE
  halo = CONV_HALO
  cw = FFN_COL_CHUNK
  t = pl.program_id(1)

  @pl.when(t == 0)
  def _():
    carry_ref[...] = jnp.zeros(carry_ref.shape, F32)

  h1 = h_ref[0]
  n = _rmsnorm(h1, gf_ref[...]).astype(BF16)
  acc = jnp.zeros((tm, D_MODEL), F32)
  for c in range(D_FF // cw):
    halves = []
    for part in range(2):
      col = part * D_FF + c * cw
      cols = slice(col, col + cw)
      slot = 2 * c + part
      buf_ref[0:halo, :] = carry_ref[slot]
      buf_ref[halo:halo + tm, :] = _dot(n, wup_ref[:, cols])
      carry_ref[slot] = buf_ref[tm:tm + halo, :]
      conv = cb_ref[:, cols]
      for tap in range(CONV_WIDTH):
        shift = CONV_WIDTH - 1 - tap
        conv = conv + cw_ref[tap:tap + 1, cols] * buf_ref[halo - shift:halo - shift + tm, :]
      halves.append(conv)
    gate, up = halves
    act = 0.5 * gate * (1.0 + lax.erf(gate * math.sqrt(0.5))) * up
    acc = acc + _dot(act.astype(BF16), wdn_ref[c * cw:(c + 1) * cw, :])
  o_ref[0] = _rmsnorm(h1 + acc, gfin_ref[...])


def _ffn(h3, g_ffn, w_up, conv_w, conv_b, w_down, g_final):
  b, s, _ = h3.shape
  full = lambda shape: pl.BlockSpec(shape, lambda bi, t: (0, 0))
  blocks = (2 * _nbytes((ROW_TILE, D_MODEL), F32) + _nbytes((D_MODEL, 2 * D_FF), BF16)
            + _nbytes((D_FF, D_MODEL), BF16) + _nbytes((CONV_WIDTH + 1, 2 * D_FF), F32))
  slots = 2 * D_FF // FFN_COL_CHUNK
  scratch = (_nbytes((ROW_TILE + CONV_HALO, FFN_COL_CHUNK), F32)
             + _nbytes((slots, CONV_HALO, FFN_COL_CHUNK), F32))
  return pl.pallas_call(
      _ffn_kernel,
      out_shape=jax.ShapeDtypeStruct((b, s, D_MODEL), F32),
      grid=(b, s // ROW_TILE),
      in_specs=[
          pl.BlockSpec((1, ROW_TILE, D_MODEL), lambda bi, t: (bi, t, 0)),
          full((1, D_MODEL)),
          full((D_MODEL, 2 * D_FF)),
          full((CONV_WIDTH, 2 * D_FF)),
          full((1, 2 * D_FF)),
          full((D_FF, D_MODEL)),
          full((1, D_MODEL)),
      ],
      out_specs=pl.BlockSpec((1, ROW_TILE, D_MODEL), lambda bi, t: (bi, t, 0)),
      scratch_shapes=[
          pltpu.VMEM((ROW_TILE + CONV_HALO, FFN_COL_CHUNK), F32),
          pltpu.VMEM((slots, CONV_HALO, FFN_COL_CHUNK), F32),
      ],
      compiler_params=_compiler_params(("parallel", "arbitrary"), blocks, scratch),
      name="ffn",
  )(h3, g_ffn, w_up, conv_w, conv_b, w_down, g_final)


def kernel(x, mem, g_mix, w_in, rel_bias, ret_gn_gain, g_mem, w_mem_kv, w_br_attn, w_br_ret,
           w_br_mem, w_out, g_ffn, w_up, conv_w, conv_b, w_down, g_final):
  b, s, d = x.shape
  assert d == D_MODEL and mem.shape == (b, MEM_LEN, D_MODEL)
  assert s % MOBA_BLOCK == 0 and s % ROW_TILE == 0 and s % RET_ROWS == 0
  assert g_mix.shape[0] == 1, "single-layer block"
  h = x.reshape(b * s, d)
  bias = _bias_tiles(rel_bias, s // MOBA_BLOCK)
  proj = _in_proj(h, g_mix, w_in[0].astype(BF16))
  proj3 = proj.reshape(b, s, D_IN)
  ya_t = _moba(proj3, bias)
  y_r = _retention(proj3, ret_gn_gain)
  kv = _mem_kv(mem, g_mem, w_mem_kv[0].astype(BF16))
  y_m = _mem_attn(proj3, kv)
  h1 = _merge(h, ya_t, y_r.reshape(b * s, RET_W), y_m.reshape(b * s, MEM_W),
              proj, w_br_attn[0].astype(BF16), w_br_ret[0].astype(BF16),
              w_br_mem[0].astype(BF16), w_out[0].astype(BF16))
  return _ffn(h1.reshape(b, s, d), g_ffn, w_up[0].astype(BF16), conv_w[0], conv_b,
              w_down[0].astype(BF16), g_final.reshape(1, d))
```

```python
import functools
import math

import jax
import jax.numpy as jnp
from jax import lax
from jax.experimental import pallas as pl
from jax.experimental.pallas import tpu as pltpu

F32 = jnp.float32
BF16 = jnp.bfloat16

D_MODEL = 1024
MEM_LEN = 256
MOBA_HEADS = 8
MOBA_HEAD_DIM = 64
MOBA_BLOCK = 256
MOBA_TOPK = 3
RET_HEADS = 4
RET_DIM = 128
RET_CHUNK = 128
ROPE_BASE = 10000.0
MEM_HEADS = 4
MEM_HEAD_DIM = 128
REL_BUCKETS = 32
REL_MAX_DIST = 2048
D_FF = 2816
CONV_WIDTH = 3
EPS = 1e-6
NEG_INF = -1e30

MOBA_W = MOBA_HEADS * MOBA_HEAD_DIM
RET_W = RET_HEADS * RET_DIM
MEM_W = MEM_HEADS * MEM_HEAD_DIM
D_IN = 3 * MOBA_W + 4 * RET_W + MEM_W + 3 * D_MODEL

V7X_LANES = 128
V7X_SUBLANES = 8
V7X_VMEM_BYTES = 64 * 1024 * 1024
V7X_VMEM_RESERVE_BYTES = 6 * 1024 * 1024

_QA_BLK = 0
_KA_BLK = MOBA_W // V7X_LANES
_VA_BLK = 2 * MOBA_W // V7X_LANES
_QR_BLK = 3 * MOBA_W // V7X_LANES
_KR_BLK = _QR_BLK + RET_W // V7X_LANES
_VR_BLK = _KR_BLK + RET_W // V7X_LANES
_GR_BLK = _VR_BLK + RET_W // V7X_LANES
_QM_COL = 3 * MOBA_W + 4 * RET_W
_Z_COL = _QM_COL + MEM_W

ROW_TILE = 512
PROJ_COL_CHUNK = 512
RET_ROWS = 512
FFN_COL_CHUNK = 256
FFN_UP_LEAD = 2
MOBA_STEP_BLOCKS = 4
CONV_HALO = V7X_SUBLANES


def _compiler_params(semantics, block_bytes, scratch_bytes=0, flags=None):
  need = 2 * block_bytes + scratch_bytes + 12 * 1024 * 1024
  limit = min(need, V7X_VMEM_BYTES - V7X_VMEM_RESERVE_BYTES)
  return pltpu.CompilerParams(dimension_semantics=semantics, vmem_limit_bytes=limit, flags=flags)


def _nbytes(shape, dtype):
  return math.prod(shape) * jnp.dtype(dtype).itemsize


def _rmsnorm(x, g):
  ms = jnp.mean(x * x, axis=-1, keepdims=True)
  return x * lax.rsqrt(ms + EPS) * g


def _dot(a, b):
  return jnp.dot(a, b, preferred_element_type=F32)


def _dot_nt(a, b):
  return lax.dot_general(a, b, (((1,), (1,)), ((), ())), preferred_element_type=F32)


def _dot_tn(a, b):
  return lax.dot_general(a, b, (((0,), (0,)), ((), ())), preferred_element_type=F32)


def _in_proj_kernel(x_ref, g_ref, w_ref, o_ref):
  n = _rmsnorm(x_ref[...], g_ref[...]).astype(BF16)
  for c in range(D_IN // PROJ_COL_CHUNK):
    cols = slice(c * PROJ_COL_CHUNK, (c + 1) * PROJ_COL_CHUNK)
    o_ref[:, cols] = _dot(n, w_ref[:, cols]).astype(BF16)


def _in_proj(x2d, g, w):
  t = x2d.shape[0]
  blocks = (_nbytes((ROW_TILE, D_MODEL), F32) + _nbytes((D_MODEL, D_IN), BF16)
            + _nbytes((ROW_TILE, D_IN), BF16))
  return pl.pallas_call(
      _in_proj_kernel,
      out_shape=jax.ShapeDtypeStruct((t, D_IN), BF16),
      grid=(t // ROW_TILE,),
      in_specs=[
          pl.BlockSpec((ROW_TILE, D_MODEL), lambda i: (i, 0)),
          pl.BlockSpec((1, D_MODEL), lambda i: (0, 0)),
          pl.BlockSpec((D_MODEL, D_IN), lambda i: (0, 0)),
      ],
      out_specs=pl.BlockSpec((ROW_TILE, D_IN), lambda i: (i, 0)),
      compiler_params=_compiler_params(("parallel",), blocks),
      name="in_proj",
  )(x2d, g, w)


def _rel_bucket(dist):
  max_exact = REL_BUCKETS // 2
  d = jnp.maximum(dist, 0)
  df = jnp.maximum(d, 1).astype(F32)
  large = max_exact + (jnp.log(df / max_exact) / math.log(REL_MAX_DIST / max_exact)
                       * (REL_BUCKETS - max_exact)).astype(jnp.int32)
  large = jnp.minimum(large, REL_BUCKETS - 1)
  return jnp.where(d < max_exact, d, large)


def _bias_kernel(tab_ref, bkt_ref, o_ref):
  h = pl.program_id(0)
  delta = pl.program_id(1)
  bkt = bkt_ref[0]
  w = jnp.zeros(bkt.shape, F32)
  for b in range(REL_BUCKETS):
    w = jnp.where(bkt == b, tab_ref[b, h], w)
  wide = jnp.broadcast_to(w, (MOBA_BLOCK, 2 * MOBA_BLOCK))
  tile = pltpu.roll(wide, MOBA_BLOCK + 1, 1, stride=1, stride_axis=0)[:, :MOBA_BLOCK]
  c = lax.broadcasted_iota(jnp.int32, tile.shape, 0)
  r = lax.broadcasted_iota(jnp.int32, tile.shape, 1)
  o_ref[0, 0] = jnp.where((delta > 0) | (r >= c), tile, NEG_INF)


def _bias_tiles(rel_bias, nb):
  m = jnp.arange(2 * MOBA_BLOCK, dtype=jnp.int32)
  dist = jnp.arange(nb, dtype=jnp.int32)[:, None] * MOBA_BLOCK + m[None, :] - (MOBA_BLOCK - 1)
  bkt = _rel_bucket(dist).reshape(nb, 1, 2 * MOBA_BLOCK)
  return pl.pallas_call(
      _bias_kernel,
      out_shape=jax.ShapeDtypeStruct((MOBA_HEADS, nb, MOBA_BLOCK, MOBA_BLOCK), F32),
      grid=(MOBA_HEADS, nb),
      in_specs=[
          pl.BlockSpec(memory_space=pltpu.SMEM),
          pl.BlockSpec((1, 1, 2 * MOBA_BLOCK), lambda h, d: (d, 0, 0)),
      ],
      out_specs=pl.BlockSpec((1, 1, MOBA_BLOCK, MOBA_BLOCK), lambda h, d: (h, d, 0, 0)),
      compiler_params=_compiler_params(
          ("parallel", "parallel"), _nbytes((MOBA_BLOCK, MOBA_BLOCK), F32)),
      name="bias_tiles",
  )(rel_bias, bkt)


def _moba_kernel(q_ref, qall_ref, k_ref, v_ref, bias_ref, o_ref, pen_ref, s_ref, *, nb):
  i = pl.program_id(2)
  blk = MOBA_BLOCK
  dh = MOBA_HEAD_DIM
  heads = V7X_LANES // dh
  seq = nb * blk
  sub = V7X_SUBLANES

  def head_lanes(hh, shape):
    lane = lax.broadcasted_iota(jnp.int32, shape, len(shape) - 1)
    return (lane >= hh * dh) & (lane < (hh + 1) * dh)

  @pl.when(i == 0)
  def _():
    kmean = jnp.concatenate(
        [jnp.mean(k_ref[0, j * blk:(j + 1) * blk, :].astype(F32), axis=0, keepdims=True)
         for j in range(nb)], axis=0)
    q_all = qall_ref[0].astype(F32)
    jidx = lax.broadcasted_iota(jnp.int32, (nb, seq), 0)
    qblk = lax.broadcasted_iota(jnp.int32, (nb, seq), 1) // blk
    past = jidx < qblk
    for hh in range(heads):
      km = jnp.where(head_lanes(hh, kmean.shape), kmean, 0.0)
      gate = lax.dot_general(km, q_all, (((1,), (1,)), ((), ())),
                             precision=lax.Precision.HIGHEST, preferred_element_type=F32)
      gate = jnp.where(past, gate, NEG_INF)
      rank = jnp.zeros(gate.shape, F32)
      for jp in range(nb):
        row = gate[jp:jp + 1, :]
        ahead = jnp.where(row > gate, 1.0, jnp.where((row == gate) & (jidx > jp), 1.0, 0.0))
        rank = rank + ahead
      pen_ref[hh] = jnp.where(past, jnp.where(rank < MOBA_TOPK, 0.0, NEG_INF),
                              jnp.where(jidx == qblk, 0.0, NEG_INF))

  q = q_ref[0].astype(F32) * (dh ** -0.5)
  qz = [jnp.where(head_lanes(hh, q.shape), q, 0.0).astype(BF16) for hh in range(heads)]
  qcols = pl.ds(pl.multiple_of(i * blk, blk), blk)

  step = MOBA_STEP_BLOCKS * blk
  trips = (i + MOBA_STEP_BLOCKS) // MOBA_STEP_BLOCKS

  def fold(x):
    return x.reshape(blk // sub, sub, blk)

  def score_pass(t, run_max):
    start = pl.multiple_of(t * step, step)
    k = k_ref[0, pl.ds(start, step), :]
    out = []
    for hh in range(heads):
      s = _dot_nt(k, qz[hh])
      mx = run_max[hh]
      for u in range(MOBA_STEP_BLOCKS):
        j = t * MOBA_STEP_BLOCKS + u
        part = s[u * blk:(u + 1) * blk] + bias_ref[hh, jnp.maximum(i - j, 0)]
        s_ref[hh, pl.ds(start + u * blk, blk), :] = part
        mx = jnp.maximum(mx, jnp.max(fold(part), axis=0) + pen_ref[hh, pl.ds(j, 1), qcols])
      out.append(mx)
    return tuple(out)

  run_max = lax.fori_loop(0, trips, score_pass,
                          tuple(jnp.full((sub, blk), NEG_INF, F32) for _ in range(heads)))
  row_max = [jnp.max(mx, axis=0, keepdims=True) for mx in run_max]

  def value_pass(t, carry):
    start = pl.multiple_of(t * step, step)
    v = v_ref[0, pl.ds(start, step), :]
    out = []
    for hh in range(heads):
      lsum, acc = carry[hh]
      probs = []
      for u in range(MOBA_STEP_BLOCKS):
        j = t * MOBA_STEP_BLOCKS + u
        shift = row_max[hh] - pen_ref[hh, pl.ds(j, 1), qcols]
        p = jnp.exp(s_ref[hh, pl.ds(start + u * blk, blk), :] - shift)
        lsum = lsum + jnp.sum(fold(p), axis=0)
        probs.append(p.astype(BF16))
      pv = _dot_tn(v, jnp.concatenate(probs, axis=0))
      out.append((lsum, acc + pv[hh * dh:(hh + 1) * dh]))
    return tuple(out)

  final = lax.fori_loop(
      0, trips, value_pass,
      tuple((jnp.zeros((sub, blk), F32), jnp.zeros((dh, blk), F32)) for _ in range(heads)))
  for hh, (lsum, acc) in enumerate(final):
    out = acc / jnp.sum(lsum, axis=0, keepdims=True)
    o_ref[0, hh * dh:(hh + 1) * dh, :] = out.astype(o_ref.dtype)


def _moba(proj3, bias):
  b, s, _ = proj3.shape
  nb = s // MOBA_BLOCK
  assert nb % MOBA_STEP_BLOCKS == 0, "the inner loops may touch blocks past the own block"
  pairs = MOBA_W // V7X_LANES
  per_pair = V7X_LANES // MOBA_HEAD_DIM
  blocks = (2 * _nbytes((MOBA_BLOCK, V7X_LANES), BF16) + 3 * _nbytes((s, V7X_LANES), BF16)
            + _nbytes((per_pair, nb, MOBA_BLOCK, MOBA_BLOCK), F32))
  scratch = _nbytes((per_pair, nb, s), F32) + _nbytes((per_pair, s, MOBA_BLOCK), F32)
  seq_blk = lambda off: pl.BlockSpec((1, s, V7X_LANES), lambda bi, hp, i: (bi, 0, off + hp))
  return pl.pallas_call(
      functools.partial(_moba_kernel, nb=nb),
      out_shape=jax.ShapeDtypeStruct((b, MOBA_W, s), BF16),
      grid=(b, pairs, nb),
      in_specs=[
          pl.BlockSpec((1, MOBA_BLOCK, V7X_LANES), lambda bi, hp, i: (bi, i, _QA_BLK + hp)),
          seq_blk(_QA_BLK), seq_blk(_KA_BLK), seq_blk(_VA_BLK),
          pl.BlockSpec((per_pair, nb, MOBA_BLOCK, MOBA_BLOCK), lambda bi, hp, i: (hp, 0, 0, 0)),
      ],
      out_specs=pl.BlockSpec((1, V7X_LANES, MOBA_BLOCK), lambda bi, hp, i: (bi, hp, i)),
      scratch_shapes=[
          pltpu.VMEM((per_pair, nb, s), F32),
          pltpu.VMEM((per_pair, s, MOBA_BLOCK), F32),
      ],
      compiler_params=_compiler_params(("parallel", "parallel", "arbitrary"), blocks, scratch),
      name="moba",
  )(proj3, proj3, proj3, proj3, bias)


def _retention_kernel(q_ref, k_ref, v_ref, g_ref, cos_ref, sin_ref, decay_ref, qin_ref, kout_ref,
                      cdec_ref, gain_ref, o_ref, state_ref):
  c_len = RET_CHUNK
  half = RET_DIM // 2

  @pl.when(pl.program_id(1) == 0)
  def _():
    state_ref[...] = jnp.zeros(state_ref.shape, F32)

  for c in range(RET_ROWS // c_len):
    rows = slice(c * c_len, (c + 1) * c_len)
    cos = cos_ref[rows, :]
    sin = sin_ref[rows, :]
    for h in range(RET_HEADS):
      lanes = slice(h * RET_DIM, (h + 1) * RET_DIM)
      q = q_ref[0, rows, lanes].astype(F32)
      k = k_ref[0, rows, lanes].astype(F32)
      v = v_ref[0, rows, lanes]
      qr = q * cos + pltpu.roll(q, half, 1) * sin
      kr = (k * cos + pltpu.roll(k, half, 1) * sin) * (RET_DIM ** -0.5)
      state = state_ref[h]
      sc = _dot_nt(qr.astype(BF16), kr.astype(BF16)) * decay_ref[h]
      inner = _dot(sc.astype(BF16), v)
      cross = _dot((qr * qin_ref[h]).astype(BF16), state.astype(BF16))
      kv = _dot_tn((kr * kout_ref[h]).astype(BF16), v)
      state_ref[h] = cdec_ref[h] * state + kv
      y = inner + cross
      mu = jnp.mean(y, axis=-1, keepdims=True)
      yc = y - mu
      var = jnp.mean(yc * yc, axis=-1, keepdims=True)
      yn = yc * lax.rsqrt(var + EPS) * gain_ref[:, lanes]
      g = g_ref[0, rows, lanes].astype(F32)
      o_ref[0, rows, lanes] = (yn * (g * jax.nn.sigmoid(g))).astype(o_ref.dtype)


def _retention(proj3, gn_gain):
  b, s, _ = proj3.shape
  c_len = RET_CHUNK
  half = RET_DIM // 2
  pos = jnp.arange(s, dtype=F32)
  inv = ROPE_BASE ** (-jnp.arange(half, dtype=F32) / half)
  ang = pos[:, None] * inv
  cos = jnp.concatenate([jnp.cos(ang), jnp.cos(ang)], axis=-1)
  sin = jnp.concatenate([-jnp.sin(ang), jnp.sin(ang)], axis=-1)
  log_gamma = jnp.log1p(-jnp.power(2.0, -5.0 - jnp.arange(RET_HEADS, dtype=F32)))
  idx = jnp.arange(c_len, dtype=F32)
  diff = idx[:, None] - idx[None, :]
  decay = jnp.where(diff >= 0, jnp.exp(log_gamma[:, None, None] * jnp.maximum(diff, 0.0)), 0.0)
  lanes = (RET_HEADS, c_len, RET_DIM)
  q_in = jnp.broadcast_to(jnp.exp(log_gamma[:, None] * (idx + 1.0))[:, :, None], lanes)
  k_out = jnp.broadcast_to(jnp.exp(log_gamma[:, None] * (c_len - 1.0 - idx))[:, :, None], lanes)
  cdec = jnp.broadcast_to(jnp.exp(log_gamma * c_len)[:, None, None], (RET_HEADS, 1, RET_DIM))

  first = _QR_BLK * V7X_LANES // RET_W
  row_blk = lambda off: pl.BlockSpec((1, RET_ROWS, RET_W), lambda bi, t: (bi, t, first + off))
  head_blk = lambda shape: pl.BlockSpec((RET_HEADS,) + shape, lambda bi, t: (0, 0, 0))
  blocks = (5 * _nbytes((RET_ROWS, RET_W), BF16) + 2 * _nbytes((RET_ROWS, RET_DIM), F32)
            + 3 * _nbytes((RET_HEADS, c_len, RET_DIM), F32))
  return pl.pallas_call(
      _retention_kernel,
      out_shape=jax.ShapeDtypeStruct((b, s, RET_W), BF16),
      grid=(b, s // RET_ROWS),
      in_specs=[
          row_blk(0), row_blk(1), row_blk(2), row_blk(3),
          pl.BlockSpec((RET_ROWS, RET_DIM), lambda bi, t: (t, 0)),
          pl.BlockSpec((RET_ROWS, RET_DIM), lambda bi, t: (t, 0)),
          head_blk((c_len, c_len)), head_blk((c_len, RET_DIM)), head_blk((c_len, RET_DIM)),
          head_blk((1, RET_DIM)),
          pl.BlockSpec((1, RET_W), lambda bi, t: (0, 0)),
      ],
      out_specs=pl.BlockSpec((1, RET_ROWS, RET_W), lambda bi, t: (bi, t, 0)),
      scratch_shapes=[pltpu.VMEM((RET_HEADS, RET_DIM, RET_DIM), F32)],
      compiler_params=_compiler_params(("parallel", "arbitrary"), blocks,
                                       _nbytes((RET_HEADS, RET_DIM, RET_DIM), F32)),
      name="retention",
  )(proj3, proj3, proj3, proj3, cos, sin, decay, q_in, k_out, cdec, gn_gain)


def _mem_kv_kernel(m_ref, g_ref, w_ref, o_ref):
  n = _rmsnorm(m_ref[0], g_ref[...]).astype(BF16)
  o_ref[0] = _dot(n, w_ref[...]).astype(BF16)


def _mem_kv(mem, g, w):
  b = mem.shape[0]
  blocks = (_nbytes((MEM_LEN, D_MODEL), F32) + _nbytes((D_MODEL, 2 * MEM_W), BF16)
            + _nbytes((MEM_LEN, 2 * MEM_W), BF16))
  return pl.pallas_call(
      _mem_kv_kernel,
      out_shape=jax.ShapeDtypeStruct((b, MEM_LEN, 2 * MEM_W), BF16),
      grid=(b,),
      in_specs=[
          pl.BlockSpec((1, MEM_LEN, D_MODEL), lambda i: (i, 0, 0)),
          pl.BlockSpec((1, D_MODEL), lambda i: (0, 0)),
          pl.BlockSpec((D_MODEL, 2 * MEM_W), lambda i: (0, 0)),
      ],
      out_specs=pl.BlockSpec((1, MEM_LEN, 2 * MEM_W), lambda i: (i, 0, 0)),
      compiler_params=_compiler_params(("parallel",), blocks),
      name="mem_kv",
  )(mem, g, w)


def _mem_attn_kernel(q_ref, kv_ref, o_ref):
  dh = MEM_HEAD_DIM
  outs = []
  for h in range(MEM_HEADS):
    q = q_ref[0, :, h * dh:(h + 1) * dh]
    k = kv_ref[0, :, h * dh:(h + 1) * dh]
    v = kv_ref[0, :, MEM_W + h * dh:MEM_W + (h + 1) * dh]
    s = _dot_nt(q, k) * (dh ** -0.5)
    p = jnp.exp(s - jnp.max(s, axis=-1, keepdims=True))
    l = jnp.sum(p, axis=-1, keepdims=True)
    outs.append(_dot(p.astype(BF16), v) / l)
  o_ref[0] = jnp.concatenate(outs, axis=1).astype(o_ref.dtype)


def _mem_attn(proj3, kv):
  b, s, _ = proj3.shape
  blocks = (2 * _nbytes((ROW_TILE, MEM_W), BF16) + _nbytes((MEM_LEN, 2 * MEM_W), BF16))
  return pl.pallas_call(
      _mem_attn_kernel,
      out_shape=jax.ShapeDtypeStruct((b, s, MEM_W), BF16),
      grid=(b, s // ROW_TILE),
      in_specs=[
          pl.BlockSpec((1, ROW_TILE, MEM_W), lambda bi, t: (bi, t, _QM_COL // MEM_W)),
          pl.BlockSpec((1, MEM_LEN, 2 * MEM_W), lambda bi, t: (bi, 0, 0)),
      ],
      out_specs=pl.BlockSpec((1, ROW_TILE, MEM_W), lambda bi, t: (bi, t, 0)),
      compiler_params=_compiler_params(("parallel", "parallel"), blocks),
      name="mem_attn",
  )(proj3, kv)


def _merge_kernel(x_ref, ya_ref, yr_ref, ym_ref, za_ref, zr_ref, zm_ref, wa_ref, wr_ref, wm_ref,
                  wo_ref, o_ref):
  def gate(z_ref, proj):
    return jax.nn.sigmoid(z_ref[...].astype(F32)) * proj

  merged = (gate(za_ref, _dot_tn(ya_ref[0], wa_ref[...]))
            + gate(zr_ref, _dot(yr_ref[...], wr_ref[...]))
            + gate(zm_ref, _dot(ym_ref[...], wm_ref[...])))
  o_ref[...] = x_ref[...] + _dot(merged.astype(BF16), wo_ref[...])


def _merge(x2d, ya_t, yr, ym, proj2d, wa, wr, wm, wo):
  t = x2d.shape[0]
  tiles_per_seq = ya_t.shape[2] // ROW_TILE
  zblk = _Z_COL // D_MODEL
  row = lambda w, col=0: pl.BlockSpec((ROW_TILE, w), lambda i: (i, col))
  ya_spec = pl.BlockSpec((1, MOBA_W, ROW_TILE),
                         lambda i: (i // tiles_per_seq, 0, i % tiles_per_seq))
  full = lambda shape: pl.BlockSpec(shape, lambda i: (0, 0))
  blocks = (2 * _nbytes((ROW_TILE, D_MODEL), F32) + 3 * _nbytes((ROW_TILE, MOBA_W), BF16)
            + 3 * _nbytes((ROW_TILE, D_MODEL), BF16) + 3 * _nbytes((MOBA_W, D_MODEL), BF16)
            + _nbytes((D_MODEL, D_MODEL), BF16))
  return pl.pallas_call(
      _merge_kernel,
      out_shape=jax.ShapeDtypeStruct((t, D_MODEL), F32),
      grid=(t // ROW_TILE,),
      in_specs=[
          row(D_MODEL), ya_spec, row(RET_W), row(MEM_W),
          row(D_MODEL, zblk), row(D_MODEL, zblk + 1), row(D_MODEL, zblk + 2),
          full((MOBA_W, D_MODEL)), full((RET_W, D_MODEL)), full((MEM_W, D_MODEL)),
          full((D_MODEL, D_MODEL)),
      ],
      out_specs=row(D_MODEL),
      compiler_params=_compiler_params(("parallel",), blocks),
      name="merge",
  )(x2d, ya_t, yr, ym, proj2d, proj2d, proj2d, wa, wr, wm, wo)


def _ffn_kernel(h_ref, gf_ref, wup_ref, cw_ref, cb_ref, wdn_ref, gfin_ref, o_ref, hid_ref,
                stage_ref):
  tm = ROW_TILE
  halo = CONV_HALO
  cw = FFN_COL_CHUNK
  n_chunks = D_FF // cw

  @pl.when(pl.program_id(1) == 0)
  def _():
    hid_ref[:, 0:halo, :] = jnp.zeros((2 * n_chunks, halo, cw), F32)

  h1 = h_ref[0]
  n = _rmsnorm(h1, gf_ref[...]).astype(BF16)

  for c in range(n_chunks):
    for part in range(2):
      col = part * D_FF + c * cw
      hid_ref[2 * c + part, halo:halo + tm, :] = _dot(n, wup_ref[:, col:col + cw])

  acc = jnp.zeros((tm, D_MODEL), F32)
  for c in range(n_chunks):
    halves = []
    for part in range(2):
      col = part * D_FF + c * cw
      cols = slice(col, col + cw)
      buf = hid_ref.at[2 * c + part]
      conv = cb_ref[:, cols]
      for tap in range(CONV_WIDTH):
        shift = CONV_WIDTH - 1 - tap
        conv = conv + cw_ref[tap:tap + 1, cols] * buf[halo - shift:halo - shift + tm, :]
      buf[0:halo, :] = buf[tm:tm + halo, :]
      halves.append(conv)
    gate, up = halves
    act = 0.5 * gate * (1.0 + lax.erf(gate * math.sqrt(0.5))) * up
    acc = acc + _dot(act.astype(BF16), wdn_ref[c * cw:(c + 1) * cw, :])

  o_ref[0] = _rmsnorm(h1 + acc, gfin_ref[...])


def _ffn(h3, g_ffn, w_up, conv_w, conv_b, w_down, g_final):
  b, s, _ = h3.shape
  full = lambda shape: pl.BlockSpec(shape, lambda bi, t: (0, 0))
  once = lambda shape: pl.BlockSpec(shape, lambda bi, t: (0, 0), pipeline_mode=pl.Buffered(1))
  tiles = 2 * _nbytes((ROW_TILE, D_MODEL), F32) + _nbytes((CONV_WIDTH + 1, 2 * D_FF), F32)
  weights = _nbytes((D_MODEL, 2 * D_FF), BF16) + _nbytes((D_FF, D_MODEL), BF16)
  slots = 2 * D_FF // FFN_COL_CHUNK
  scratch = (_nbytes((slots, ROW_TILE + CONV_HALO, FFN_COL_CHUNK), F32)
             + _nbytes((ROW_TILE, D_FF), BF16))
  return pl.pallas_call(
      _ffn_kernel,
      out_shape=jax.ShapeDtypeStruct((b, s, D_MODEL), F32),
      grid=(b, s // ROW_TILE),
      in_specs=[
          pl.BlockSpec((1, ROW_TILE, D_MODEL), lambda bi, t: (bi, t, 0)),
          full((1, D_MODEL)),
          once((D_MODEL, 2 * D_FF)),
          full((CONV_WIDTH, 2 * D_FF)),
          full((1, 2 * D_FF)),
          once((D_FF, D_MODEL)),
          full((1, D_MODEL)),
      ],
      out_specs=pl.BlockSpec((1, ROW_TILE, D_MODEL), lambda bi, t: (bi, t, 0)),
      scratch_shapes=[
          pltpu.VMEM((slots, ROW_TILE + CONV_HALO, FFN_COL_CHUNK), F32),
          pltpu.VMEM((ROW_TILE, D_MODEL + D_FF), BF16),
      ],
      compiler_params=_compiler_params(("parallel", "arbitrary"), tiles, weights + scratch),
      name="ffn",
  )(h3, g_ffn, w_up, conv_w, conv_b, w_down, g_final)


def kernel(x, mem, g_mix, w_in, rel_bias, ret_gn_gain, g_mem, w_mem_kv, w_br_attn, w_br_ret,
           w_br_mem, w_out, g_ffn, w_up, conv_w, conv_b, w_down, g_final):
  b, s, d = x.shape
  assert d == D_MODEL and mem.shape == (b, MEM_LEN, D_MODEL)
  assert s % MOBA_BLOCK == 0 and s % ROW_TILE == 0 and s % RET_ROWS == 0
  assert g_mix.shape[0] == 1, "single-layer block"
  h = x.reshape(b * s, d)
  bias = _bias_tiles(rel_bias, s // MOBA_BLOCK)
  proj = _in_proj(h, g_mix, w_in[0].astype(BF16))
  proj3 = proj.reshape(b, s, D_IN)
  ya_t = _moba(proj3, bias)
  y_r = _retention(proj3, ret_gn_gain)
  kv = _mem_kv(mem, g_mem, w_mem_kv[0].astype(BF16))
  y_m = _mem_attn(proj3, kv)
  h1 = _merge(h, ya_t, y_r.reshape(b * s, RET_W), y_m.reshape(b * s, MEM_W),
              proj, w_br_attn[0].astype(BF16), w_br_ret[0].astype(BF16),
              w_br_mem[0].astype(BF16), w_out[0].astype(BF16))
  return _ffn(h1.reshape(b, s, d), g_ffn, w_up[0].astype(BF16), conv_w[0], conv_b,
              w_down[0].astype(BF16), g_final.reshape(1, d))
```

```python
import functools
import math

import jax
import jax.numpy as jnp
from jax import lax
from jax.experimental import pallas as pl
from jax.experimental.pallas import tpu as pltpu

F32 = jnp.float32
BF16 = jnp.bfloat16

D_MODEL = 1024
MEM_LEN = 256
MOBA_HEADS = 8
MOBA_HEAD_DIM = 64
MOBA_BLOCK = 256
MOBA_TOPK = 3
RET_HEADS = 4
RET_DIM = 128
RET_CHUNK = 128
ROPE_BASE = 10000.0
MEM_HEADS = 4
MEM_HEAD_DIM = 128
REL_BUCKETS = 32
REL_MAX_DIST = 2048
D_FF = 2816
CONV_WIDTH = 3
EPS = 1e-6
NEG_INF = -1e30
LOG2E = math.log2(math.e)

MOBA_W = MOBA_HEADS * MOBA_HEAD_DIM
RET_W = RET_HEADS * RET_DIM
MEM_W = MEM_HEADS * MEM_HEAD_DIM
D_IN = 3 * MOBA_W + 4 * RET_W + MEM_W + 3 * D_MODEL

V7X_LANES = 128
V7X_SUBLANES = 8
V7X_VMEM_BYTES = 64 * 1024 * 1024
V7X_VMEM_RESERVE_BYTES = 6 * 1024 * 1024

_QA_BLK = 0
_KA_BLK = MOBA_W // V7X_LANES
_VA_BLK = 2 * MOBA_W // V7X_LANES
_QR_BLK = 3 * MOBA_W // V7X_LANES
_KR_BLK = _QR_BLK + RET_W // V7X_LANES
_VR_BLK = _KR_BLK + RET_W // V7X_LANES
_GR_BLK = _VR_BLK + RET_W // V7X_LANES
_QM_COL = 3 * MOBA_W + 4 * RET_W
_Z_COL = _QM_COL + MEM_W

ROW_TILE = 512
PROJ_COL_CHUNK = 512
RET_ROWS = 512
FFN_COL_CHUNK = 256
FFN_UP_LEAD = 2
MOBA_STEP_BLOCKS = 4
MOBA_Q_BLOCKS = 2
CONV_HALO = V7X_SUBLANES


def _compiler_params(semantics, block_bytes, scratch_bytes=0, flags=None):
  need = 2 * block_bytes + scratch_bytes + 12 * 1024 * 1024
  limit = min(need, V7X_VMEM_BYTES - V7X_VMEM_RESERVE_BYTES)
  return pltpu.CompilerParams(dimension_semantics=semantics, vmem_limit_bytes=limit, flags=flags)


def _nbytes(shape, dtype):
  return math.prod(shape) * jnp.dtype(dtype).itemsize


def _rmsnorm(x, g):
  ms = jnp.mean(x * x, axis=-1, keepdims=True)
  return x * lax.rsqrt(ms + EPS) * g


def _dot(a, b):
  return jnp.dot(a, b, preferred_element_type=F32)


def _dot_nt(a, b):
  return lax.dot_general(a, b, (((1,), (1,)), ((), ())), preferred_element_type=F32)


def _dot_tn(a, b):
  return lax.dot_general(a, b, (((0,), (0,)), ((), ())), preferred_element_type=F32)


def _in_proj_kernel(x_ref, g_ref, w_ref, o_ref):
  n = _rmsnorm(x_ref[...], g_ref[...]).astype(BF16)
  for c in range(D_IN // PROJ_COL_CHUNK):
    cols = slice(c * PROJ_COL_CHUNK, (c + 1) * PROJ_COL_CHUNK)
    o_ref[:, cols] = _dot(n, w_ref[:, cols]).astype(BF16)


def _in_proj(x2d, g, w):
  t = x2d.shape[0]
  blocks = (_nbytes((ROW_TILE, D_MODEL), F32) + _nbytes((D_MODEL, D_IN), BF16)
            + _nbytes((ROW_TILE, D_IN), BF16))
  return pl.pallas_call(
      _in_proj_kernel,
      out_shape=jax.ShapeDtypeStruct((t, D_IN), BF16),
      grid=(t // ROW_TILE,),
      in_specs=[
          pl.BlockSpec((ROW_TILE, D_MODEL), lambda i: (i, 0)),
          pl.BlockSpec((1, D_MODEL), lambda i: (0, 0)),
          pl.BlockSpec((D_MODEL, D_IN), lambda i: (0, 0)),
      ],
      out_specs=pl.BlockSpec((ROW_TILE, D_IN), lambda i: (i, 0)),
      compiler_params=_compiler_params(("parallel",), blocks),
      name="in_proj",
  )(x2d, g, w)


def _rel_bucket(dist):
  max_exact = REL_BUCKETS // 2
  d = jnp.maximum(dist, 0)
  df = jnp.maximum(d, 1).astype(F32)
  large = max_exact + (jnp.log(df / max_exact) / math.log(REL_MAX_DIST / max_exact)
                       * (REL_BUCKETS - max_exact)).astype(jnp.int32)
  large = jnp.minimum(large, REL_BUCKETS - 1)
  return jnp.where(d < max_exact, d, large)


def _bias_kernel(tab_ref, bkt_ref, o_ref):
  h = pl.program_id(0)
  delta = pl.program_id(1)
  bkt = bkt_ref[0]
  w = jnp.zeros(bkt.shape, F32)
  for b in range(REL_BUCKETS):
    w = jnp.where(bkt == b, tab_ref[b, h], w)
  wide = jnp.broadcast_to(w, (MOBA_BLOCK, 2 * MOBA_BLOCK))
  tile = pltpu.roll(wide, MOBA_BLOCK + 1, 1, stride=1, stride_axis=0)[:, :MOBA_BLOCK]
  c = lax.broadcasted_iota(jnp.int32, tile.shape, 0)
  r = lax.broadcasted_iota(jnp.int32, tile.shape, 1)
  o_ref[0, 0] = jnp.where((delta > 0) | (r >= c), tile * LOG2E, NEG_INF)


def _bias_tiles(rel_bias, nb):
  m = jnp.arange(2 * MOBA_BLOCK, dtype=jnp.int32)
  dist = jnp.arange(nb, dtype=jnp.int32)[:, None] * MOBA_BLOCK + m[None, :] - (MOBA_BLOCK - 1)
  bkt = _rel_bucket(dist).reshape(nb, 1, 2 * MOBA_BLOCK)
  return pl.pallas_call(
      _bias_kernel,
      out_shape=jax.ShapeDtypeStruct((MOBA_HEADS, nb, MOBA_BLOCK, MOBA_BLOCK), F32),
      grid=(MOBA_HEADS, nb),
      in_specs=[
          pl.BlockSpec(memory_space=pltpu.SMEM),
          pl.BlockSpec((1, 1, 2 * MOBA_BLOCK), lambda h, d: (d, 0, 0)),
      ],
      out_specs=pl.BlockSpec((1, 1, MOBA_BLOCK, MOBA_BLOCK), lambda h, d: (h, d, 0, 0)),
      compiler_params=_compiler_params(
          ("parallel", "parallel"), _nbytes((MOBA_BLOCK, MOBA_BLOCK), F32)),
      name="bias_tiles",
  )(rel_bias, bkt)


def _moba_kernel(q_ref, qall_ref, k_ref, v_ref, bias_ref, o_ref, pen_ref, s_ref, *, nb):
  i = pl.program_id(2)
  blk = MOBA_BLOCK
  dh = MOBA_HEAD_DIM
  heads = V7X_LANES // dh
  assert heads == 2, "each head borrows the other head's lanes for its probability sums"
  seq = nb * blk
  sub = V7X_SUBLANES

  def head_lanes(hh, shape):
    lane = lax.broadcasted_iota(jnp.int32, shape, len(shape) - 1)
    return (lane >= hh * dh) & (lane < (hh + 1) * dh)

  @pl.when(i == 0)
  def _():
    kmean = jnp.concatenate(
        [jnp.mean(k_ref[0, j * blk:(j + 1) * blk, :].astype(F32), axis=0, keepdims=True)
         for j in range(nb)], axis=0)
    q_all = qall_ref[0].astype(F32)
    jidx = lax.broadcasted_iota(jnp.int32, (nb, seq), 0)
    qblk = lax.broadcasted_iota(jnp.int32, (nb, seq), 1) // blk
    past = jidx < qblk
    for hh in range(heads):
      km = jnp.where(head_lanes(hh, kmean.shape), kmean, 0.0)
      gate = lax.dot_general(km, q_all, (((1,), (1,)), ((), ())),
                             precision=lax.Precision.HIGHEST, preferred_element_type=F32)
      gate = jnp.where(past, gate, NEG_INF)
      rank = jnp.zeros(gate.shape, F32)
      for jp in range(nb):
        row = gate[jp:jp + 1, :]
        ahead = jnp.where(row > gate, 1.0, jnp.where((row == gate) & (jidx > jp), 1.0, 0.0))
        rank = rank + ahead
      pen_ref[hh] = jnp.where(past, jnp.where(rank < MOBA_TOPK, 0.0, NEG_INF),
                              jnp.where(jidx == qblk, 0.0, NEG_INF))

  qt = MOBA_Q_BLOCKS * blk
  q = q_ref[0].astype(F32) * (dh ** -0.5)
  qz = [jnp.where(head_lanes(hh, q.shape), q, 0.0).astype(BF16) for hh in range(heads)]
  qcols = pl.ds(pl.multiple_of(i * qt, qt), qt)
  first_qblk = i * MOBA_Q_BLOCKS

  step = MOBA_STEP_BLOCKS * blk
  trips = (first_qblk + MOBA_Q_BLOCKS - 1 + MOBA_STEP_BLOCKS) // MOBA_STEP_BLOCKS

  def fold(x):
    return x.reshape(blk // sub, sub, qt)

  def score_pass(t, run_max):
    start = pl.multiple_of(t * step, step)
    k = k_ref[0, pl.ds(start, step), :]
    out = []
    for hh in range(heads):
      s = _dot_nt(k, qz[hh])
      mx = run_max[hh]
      for u in range(MOBA_STEP_BLOCKS):
        j = t * MOBA_STEP_BLOCKS + u
        bias = jnp.concatenate(
            [bias_ref[hh, jnp.maximum(first_qblk + qb - j, 0)] for qb in range(MOBA_Q_BLOCKS)],
            axis=1)
        part = s[u * blk:(u + 1) * blk] * LOG2E + bias
        s_ref[hh, pl.ds(start + u * blk, blk), :] = part
        mx = jnp.maximum(mx, jnp.max(fold(part), axis=0) + pen_ref[hh, pl.ds(j, 1), qcols])
      out.append(mx)
    return tuple(out)

  run_max = lax.fori_loop(0, trips, score_pass,
                          tuple(jnp.full((sub, qt), NEG_INF, F32) for _ in range(heads)))
  row_max = [jnp.max(mx, axis=0, keepdims=True) for mx in run_max]

  def value_pass(t, carry):
    start = pl.multiple_of(t * step, step)
    v = v_ref[0, pl.ds(start, step), :]
    out = []
    for hh in range(heads):
      lsum, acc = carry[hh]
      probs = []
      for u in range(MOBA_STEP_BLOCKS):
        j = t * MOBA_STEP_BLOCKS + u
        shift = row_max[hh] - pen_ref[hh, pl.ds(j, 1), qcols]
        p = jnp.exp2(s_ref[hh, pl.ds(start + u * blk, blk), :] - shift)
        probs.append(p.astype(BF16))
      v_ones = jnp.where(head_lanes(hh, v.shape), v, jnp.ones_like(v))
      pv = _dot_tn(v_ones, jnp.concatenate(probs, axis=0))
      other = (heads - 1 - hh) * dh
      out.append((lsum + pv[other:other + sub], acc + pv[hh * dh:(hh + 1) * dh]))
    return tuple(out)

  final = lax.fori_loop(
      0, trips, value_pass,
      tuple((jnp.zeros((sub, qt), F32), jnp.zeros((dh, qt), F32)) for _ in range(heads)))
  for hh, (lsum, acc) in enumerate(final):
    o_ref[0, hh * dh:(hh + 1) * dh, :] = (acc / lsum[0:1]).astype(o_ref.dtype)


def _moba(proj3, bias):
  b, s, _ = proj3.shape
  nb = s // MOBA_BLOCK
  assert nb % MOBA_STEP_BLOCKS == 0, "the inner loops may touch blocks past the own block"
  assert nb % MOBA_Q_BLOCKS == 0
  pairs = MOBA_W // V7X_LANES
  per_pair = V7X_LANES // MOBA_HEAD_DIM
  qt = MOBA_Q_BLOCKS * MOBA_BLOCK
  blocks = (2 * _nbytes((qt, V7X_LANES), BF16) + 3 * _nbytes((s, V7X_LANES), BF16)
            + _nbytes((per_pair, nb, MOBA_BLOCK, MOBA_BLOCK), F32))
  scratch = _nbytes((per_pair, nb, s), F32) + _nbytes((per_pair, s, qt), F32)
  seq_blk = lambda off: pl.BlockSpec((1, s, V7X_LANES), lambda bi, hp, i: (bi, 0, off + hp))
  return pl.pallas_call(
      functools.partial(_moba_kernel, nb=nb),
      out_shape=jax.ShapeDtypeStruct((b, MOBA_W, s), BF16),
      grid=(b, pairs, nb // MOBA_Q_BLOCKS),
      in_specs=[
          pl.BlockSpec((1, qt, V7X_LANES), lambda bi, hp, i: (bi, i, _QA_BLK + hp)),
          seq_blk(_QA_BLK), seq_blk(_KA_BLK), seq_blk(_VA_BLK),
          pl.BlockSpec((per_pair, nb, MOBA_BLOCK, MOBA_BLOCK), lambda bi, hp, i: (hp, 0, 0, 0)),
      ],
      out_specs=pl.BlockSpec((1, V7X_LANES, qt), lambda bi, hp, i: (bi, hp, i)),
      scratch_shapes=[
          pltpu.VMEM((per_pair, nb, s), F32),
          pltpu.VMEM((per_pair, s, qt), F32),
      ],
      compiler_params=_compiler_params(("parallel", "parallel", "arbitrary"), blocks, scratch),
      name="moba",
  )(proj3, proj3, proj3, proj3, bias)


def _retention_kernel(q_ref, k_ref, v_ref, g_ref, cos_ref, sin_ref, decay_ref, qin_ref, kout_ref,
                      cdec_ref, gain_ref, o_ref, state_ref):
  c_len = RET_CHUNK
  half = RET_DIM // 2

  @pl.when(pl.program_id(1) == 0)
  def _():
    state_ref[...] = jnp.zeros(state_ref.shape, F32)

  for c in range(RET_ROWS // c_len):
    rows = slice(c * c_len, (c + 1) * c_len)
    cos = cos_ref[rows, :]
    sin = sin_ref[rows, :]
    for h in range(RET_HEADS):
      lanes = slice(h * RET_DIM, (h + 1) * RET_DIM)
      q = q_ref[0, rows, lanes].astype(F32)
      k = k_ref[0, rows, lanes].astype(F32)
      v = v_ref[0, rows, lanes]
      qr = q * cos + pltpu.roll(q, half, 1) * sin
      kr = (k * cos + pltpu.roll(k, half, 1) * sin) * (RET_DIM ** -0.5)
      state = state_ref[h]
      sc = _dot_nt(qr.astype(BF16), kr.astype(BF16)) * decay_ref[h]
      inner = _dot(sc.astype(BF16), v)
      cross = _dot((qr * qin_ref[h]).astype(BF16), state.astype(BF16))
      kv = _dot_tn((kr * kout_ref[h]).astype(BF16), v)
      state_ref[h] = cdec_ref[h] * state + kv
      y = inner + cross
      mu = jnp.mean(y, axis=-1, keepdims=True)
      yc = y - mu
      var = jnp.mean(yc * yc, axis=-1, keepdims=True)
      yn = yc * lax.rsqrt(var + EPS) * gain_ref[:, lanes]
      g = g_ref[0, rows, lanes].astype(F32)
      o_ref[0, rows, lanes] = (yn * (g * jax.nn.sigmoid(g))).astype(o_ref.dtype)


def _retention(proj3, gn_gain):
  b, s, _ = proj3.shape
  c_len = RET_CHUNK
  half = RET_DIM // 2
  pos = jnp.arange(s, dtype=F32)
  inv = ROPE_BASE ** (-jnp.arange(half, dtype=F32) / half)
  ang = pos[:, None] * inv
  cos = jnp.concatenate([jnp.cos(ang), jnp.cos(ang)], axis=-1)
  sin = jnp.concatenate([-jnp.sin(ang), jnp.sin(ang)], axis=-1)
  log_gamma = jnp.log1p(-jnp.power(2.0, -5.0 - jnp.arange(RET_HEADS, dtype=F32)))
  idx = jnp.arange(c_len, dtype=F32)
  diff = idx[:, None] - idx[None, :]
  decay = jnp.where(diff >= 0, jnp.exp(log_gamma[:, None, None] * jnp.maximum(diff, 0.0)), 0.0)
  lanes = (RET_HEADS, c_len, RET_DIM)
  q_in = jnp.broadcast_to(jnp.exp(log_gamma[:, None] * (idx + 1.0))[:, :, None], lanes)
  k_out = jnp.broadcast_to(jnp.exp(log_gamma[:, None] * (c_len - 1.0 - idx))[:, :, None], lanes)
  cdec = jnp.broadcast_to(jnp.exp(log_gamma * c_len)[:, None, None], (RET_HEADS, 1, RET_DIM))

  first = _QR_BLK * V7X_LANES // RET_W
  row_blk = lambda off: pl.BlockSpec((1, RET_ROWS, RET_W), lambda bi, t: (bi, t, first + off))
  head_blk = lambda shape: pl.BlockSpec((RET_HEADS,) + shape, lambda bi, t: (0, 0, 0))
  blocks = (5 * _nbytes((RET_ROWS, RET_W), BF16) + 2 * _nbytes((RET_ROWS, RET_DIM), F32)
            + 3 * _nbytes((RET_HEADS, c_len, RET_DIM), F32))
  return pl.pallas_call(
      _retention_kernel,
      out_shape=jax.ShapeDtypeStruct((b, s, RET_W), BF16),
      grid=(b, s // RET_ROWS),
      in_specs=[
          row_blk(0), row_blk(1), row_blk(2), row_blk(3),
          pl.BlockSpec((RET_ROWS, RET_DIM), lambda bi, t: (t, 0)),
          pl.BlockSpec((RET_ROWS, RET_DIM), lambda bi, t: (t, 0)),
          head_blk((c_len, c_len)), head_blk((c_len, RET_DIM)), head_blk((c_len, RET_DIM)),
          head_blk((1, RET_DIM)),
          pl.BlockSpec((1, RET_W), lambda bi, t: (0, 0)),
      ],
      out_specs=pl.BlockSpec((1, RET_ROWS, RET_W), lambda bi, t: (bi, t, 0)),
      scratch_shapes=[pltpu.VMEM((RET_HEADS, RET_DIM, RET_DIM), F32)],
      compiler_params=_compiler_params(("parallel", "arbitrary"), blocks,
                                       _nbytes((RET_HEADS, RET_DIM, RET_DIM), F32)),
      name="retention",
  )(proj3, proj3, proj3, proj3, cos, sin, decay, q_in, k_out, cdec, gn_gain)


def _mem_kv_kernel(m_ref, g_ref, w_ref, o_ref):
  n = _rmsnorm(m_ref[0], g_ref[...]).astype(BF16)
  o_ref[0] = _dot(n, w_ref[...]).astype(BF16)


def _mem_kv(mem, g, w):
  b = mem.shape[0]
  blocks = (_nbytes((MEM_LEN, D_MODEL), F32) + _nbytes((D_MODEL, 2 * MEM_W), BF16)
            + _nbytes((MEM_LEN, 2 * MEM_W), BF16))
  return pl.pallas_call(
      _mem_kv_kernel,
      out_shape=jax.ShapeDtypeStruct((b, MEM_LEN, 2 * MEM_W), BF16),
      grid=(b,),
      in_specs=[
          pl.BlockSpec((1, MEM_LEN, D_MODEL), lambda i: (i, 0, 0)),
          pl.BlockSpec((1, D_MODEL), lambda i: (0, 0)),
          pl.BlockSpec((D_MODEL, 2 * MEM_W), lambda i: (0, 0)),
      ],
      out_specs=pl.BlockSpec((1, MEM_LEN, 2 * MEM_W), lambda i: (i, 0, 0)),
      compiler_params=_compiler_params(("parallel",), blocks),
      name="mem_kv",
  )(mem, g, w)


def _mem_attn_kernel(q_ref, kv_ref, o_ref):
  dh = MEM_HEAD_DIM
  outs = []
  for h in range(MEM_HEADS):
    q = q_ref[0, :, h * dh:(h + 1) * dh]
    k = kv_ref[0, :, h * dh:(h + 1) * dh]
    v = kv_ref[0, :, MEM_W + h * dh:MEM_W + (h + 1) * dh]
    s = _dot_nt(q, k) * (dh ** -0.5)
    p = jnp.exp(s - jnp.max(s, axis=-1, keepdims=True))
    l = jnp.sum(p, axis=-1, keepdims=True)
    outs.append(_dot(p.astype(BF16), v) / l)
  o_ref[0] = jnp.concatenate(outs, axis=1).astype(o_ref.dtype)


def _mem_attn(proj3, kv):
  b, s, _ = proj3.shape
  blocks = (2 * _nbytes((ROW_TILE, MEM_W), BF16) + _nbytes((MEM_LEN, 2 * MEM_W), BF16))
  return pl.pallas_call(
      _mem_attn_kernel,
      out_shape=jax.ShapeDtypeStruct((b, s, MEM_W), BF16),
      grid=(b, s // ROW_TILE),
      in_specs=[
          pl.BlockSpec((1, ROW_TILE, MEM_W), lambda bi, t: (bi, t, _QM_COL // MEM_W)),
          pl.BlockSpec((1, MEM_LEN, 2 * MEM_W), lambda bi, t: (bi, 0, 0)),
      ],
      out_specs=pl.BlockSpec((1, ROW_TILE, MEM_W), lambda bi, t: (bi, t, 0)),
      compiler_params=_compiler_params(("parallel", "parallel"), blocks),
      name="mem_attn",
  )(proj3, kv)


def _merge_kernel(x_ref, ya_ref, yr_ref, ym_ref, za_ref, zr_ref, zm_ref, wa_ref, wr_ref, wm_ref,
                  wo_ref, o_ref):
  def gate(z_ref, proj):
    return jax.nn.sigmoid(z_ref[...].astype(F32)) * proj

  merged = (gate(za_ref, _dot_tn(ya_ref[0], wa_ref[...]))
            + gate(zr_ref, _dot(yr_ref[...], wr_ref[...]))
            + gate(zm_ref, _dot(ym_ref[...], wm_ref[...])))
  o_ref[...] = x_ref[...] + _dot(merged.astype(BF16), wo_ref[...])


def _merge(x2d, ya_t, yr, ym, proj2d, wa, wr, wm, wo):
  t = x2d.shape[0]
  tiles_per_seq = ya_t.shape[2] // ROW_TILE
  zblk = _Z_COL // D_MODEL
  row = lambda w, col=0: pl.BlockSpec((ROW_TILE, w), lambda i: (i, col))
  ya_spec = pl.BlockSpec((1, MOBA_W, ROW_TILE),
                         lambda i: (i // tiles_per_seq, 0, i % tiles_per_seq))
  full = lambda shape: pl.BlockSpec(shape, lambda i: (0, 0))
  blocks = (2 * _nbytes((ROW_TILE, D_MODEL), F32) + 3 * _nbytes((ROW_TILE, MOBA_W), BF16)
            + 3 * _nbytes((ROW_TILE, D_MODEL), BF16) + 3 * _nbytes((MOBA_W, D_MODEL), BF16)
            + _nbytes((D_MODEL, D_MODEL), BF16))
  return pl.pallas_call(
      _merge_kernel,
      out_shape=jax.ShapeDtypeStruct((t, D_MODEL), F32),
      grid=(t // ROW_TILE,),
      in_specs=[
          row(D_MODEL), ya_spec, row(RET_W), row(MEM_W),
          row(D_MODEL, zblk), row(D_MODEL, zblk + 1), row(D_MODEL, zblk + 2),
          full((MOBA_W, D_MODEL)), full((RET_W, D_MODEL)), full((MEM_W, D_MODEL)),
          full((D_MODEL, D_MODEL)),
      ],
      out_specs=row(D_MODEL),
      compiler_params=_compiler_params(("parallel",), blocks),
      name="merge",
  )(x2d, ya_t, yr, ym, proj2d, proj2d, proj2d, wa, wr, wm, wo)


def _ffn_kernel(h_ref, gf_ref, wup_ref, cw_ref, cb_ref, wdn_ref, gfin_ref, o_ref, hid_ref,
                stage_ref):
  tm = ROW_TILE
  halo = CONV_HALO
  cw = FFN_COL_CHUNK
  n_chunks = D_FF // cw

  @pl.when(pl.program_id(1) == 0)
  def _():
    hid_ref[:, 0:halo, :] = jnp.zeros((2 * n_chunks, halo, cw), F32)

  h1 = h_ref[0]
  n = _rmsnorm(h1, gf_ref[...]).astype(BF16)

  for c in range(n_chunks):
    for part in range(2):
      col = part * D_FF + c * cw
      hid_ref[2 * c + part, halo:halo + tm, :] = _dot(n, wup_ref[:, col:col + cw])

  acc = jnp.zeros((tm, D_MODEL), F32)
  for c in range(n_chunks):
    halves = []
    for part in range(2):
      col = part * D_FF + c * cw
      cols = slice(col, col + cw)
      buf = hid_ref.at[2 * c + part]
      conv = cb_ref[:, cols]
      for tap in range(CONV_WIDTH):
        shift = CONV_WIDTH - 1 - tap
        conv = conv + cw_ref[tap:tap + 1, cols] * buf[halo - shift:halo - shift + tm, :]
      buf[0:halo, :] = buf[tm:tm + halo, :]
      halves.append(conv)
    gate, up = halves
    act = 0.5 * gate * (1.0 + lax.erf(gate * math.sqrt(0.5))) * up
    acc = acc + _dot(act.astype(BF16), wdn_ref[c * cw:(c + 1) * cw, :])

  o_ref[0] = _rmsnorm(h1 + acc, gfin_ref[...])


def _ffn(h3, g_ffn, w_up, conv_w, conv_b, w_down, g_final):
  b, s, _ = h3.shape
  full = lambda shape: pl.BlockSpec(shape, lambda bi, t: (0, 0))
  once = lambda shape: pl.BlockSpec(shape, lambda bi, t: (0, 0), pipeline_mode=pl.Buffered(1))
  tiles = 2 * _nbytes((ROW_TILE, D_MODEL), F32) + _nbytes((CONV_WIDTH + 1, 2 * D_FF), F32)
  weights = _nbytes((D_MODEL, 2 * D_FF), BF16) + _nbytes((D_FF, D_MODEL), BF16)
  slots = 2 * D_FF // FFN_COL_CHUNK
  scratch = (_nbytes((slots, ROW_TILE + CONV_HALO, FFN_COL_CHUNK), F32)
             + _nbytes((ROW_TILE, D_FF), BF16))
  return pl.pallas_call(
      _ffn_kernel,
      out_shape=jax.ShapeDtypeStruct((b, s, D_MODEL), F32),
      grid=(b, s // ROW_TILE),
      in_specs=[
          pl.BlockSpec((1, ROW_TILE, D_MODEL), lambda bi, t: (bi, t, 0)),
          full((1, D_MODEL)),
          once((D_MODEL, 2 * D_FF)),
          full((CONV_WIDTH, 2 * D_FF)),
          full((1, 2 * D_FF)),
          once((D_FF, D_MODEL)),
          full((1, D_MODEL)),
      ],
      out_specs=pl.BlockSpec((1, ROW_TILE, D_MODEL), lambda bi, t: (bi, t, 0)),
      scratch_shapes=[
          pltpu.VMEM((slots, ROW_TILE + CONV_HALO, FFN_COL_CHUNK), F32),
          pltpu.VMEM((ROW_TILE, D_MODEL + D_FF), BF16),
      ],
      compiler_params=_compiler_params(("parallel", "arbitrary"), tiles, weights + scratch),
      name="ffn",
  )(h3, g_ffn, w_up, conv_w, conv_b, w_down, g_final)


def kernel(x, mem, g_mix, w_in, rel_bias, ret_gn_gain, g_mem, w_mem_kv, w_br_attn, w_br_ret,
           w_br_mem, w_out, g_ffn, w_up, conv_w, conv_b, w_down, g_final):
  b, s, d = x.shape
  assert d == D_MODEL and mem.shape == (b, MEM_LEN, D_MODEL)
  assert s % MOBA_BLOCK == 0 and s % ROW_TILE == 0 and s % RET_ROWS == 0
  assert g_mix.shape[0] == 1, "single-layer block"
  h = x.reshape(b * s, d)
  bias = _bias_tiles(rel_bias, s // MOBA_BLOCK)
  proj = _in_proj(h, g_mix, w_in[0].astype(BF16))
  proj3 = proj.reshape(b, s, D_IN)
  ya_t = _moba(proj3, bias)
  y_r = _retention(proj3, ret_gn_gain)
  kv = _mem_kv(mem, g_mem, w_mem_kv[0].astype(BF16))
  y_m = _mem_attn(proj3, kv)
  h1 = _merge(h, ya_t, y_r.reshape(b * s, RET_W), y_m.reshape(b * s, MEM_W),
              proj, w_br_attn[0].astype(BF16), w_br_ret[0].astype(BF16),
              w_br_mem[0].astype(BF16), w_out[0].astype(BF16))
  return _ffn(h1.reshape(b, s, d), g_ffn, w_up[0].astype(BF16), conv_w[0], conv_b,
              w_down[0].astype(BF16), g_final.reshape(1, d))
```

```python
import functools
import math

import jax
import jax.numpy as jnp
from jax import lax
from jax.experimental import pallas as pl
from jax.experimental.pallas import tpu as pltpu

F32 = jnp.float32
BF16 = jnp.bfloat16

D_MODEL = 1024
MEM_LEN = 256
MOBA_HEADS = 8
MOBA_HEAD_DIM = 64
MOBA_BLOCK = 256
MOBA_TOPK = 3
RET_HEADS = 4
RET_DIM = 128
RET_CHUNK = 128
ROPE_BASE = 10000.0
MEM_HEADS = 4
MEM_HEAD_DIM = 128
REL_BUCKETS = 32
REL_MAX_DIST = 2048
D_FF = 2816
CONV_WIDTH = 3
EPS = 1e-6
NEG_INF = -1e30
LOG2E = math.log2(math.e)

MOBA_W = MOBA_HEADS * MOBA_HEAD_DIM
RET_W = RET_HEADS * RET_DIM
MEM_W = MEM_HEADS * MEM_HEAD_DIM
D_IN = 3 * MOBA_W + 4 * RET_W + MEM_W + 3 * D_MODEL

V7X_LANES = 128
V7X_SUBLANES = 8
V7X_VMEM_BYTES = 64 * 1024 * 1024
V7X_VMEM_RESERVE_BYTES = 6 * 1024 * 1024

_QA_BLK = 0
_KA_BLK = MOBA_W // V7X_LANES
_VA_BLK = 2 * MOBA_W // V7X_LANES
_QR_BLK = 3 * MOBA_W // V7X_LANES
_KR_BLK = _QR_BLK + RET_W // V7X_LANES
_VR_BLK = _KR_BLK + RET_W // V7X_LANES
_GR_BLK = _VR_BLK + RET_W // V7X_LANES
_QM_COL = 3 * MOBA_W + 4 * RET_W
_Z_COL = _QM_COL + MEM_W

ROW_TILE = 512
PROJ_COL_CHUNK = 512
RET_ROWS = 512
FFN_COL_CHUNK = 256
FFN_DOWN_SPLIT = 6
MOBA_STEP_BLOCKS = 4
MOBA_Q_BLOCKS = 2
CONV_HALO = V7X_SUBLANES


def _compiler_params(semantics, block_bytes, scratch_bytes=0, flags=None):
  need = 2 * block_bytes + scratch_bytes + 12 * 1024 * 1024
  limit = min(need, V7X_VMEM_BYTES - V7X_VMEM_RESERVE_BYTES)
  return pltpu.CompilerParams(dimension_semantics=semantics, vmem_limit_bytes=limit, flags=flags)


def _nbytes(shape, dtype):
  return math.prod(shape) * jnp.dtype(dtype).itemsize


def _rmsnorm(x, g):
  ms = jnp.mean(x * x, axis=-1, keepdims=True)
  return x * lax.rsqrt(ms + EPS) * g


def _dot(a, b):
  return jnp.dot(a, b, preferred_element_type=F32)


def _dot_nt(a, b):
  return lax.dot_general(a, b, (((1,), (1,)), ((), ())), preferred_element_type=F32)


def _dot_tn(a, b):
  return lax.dot_general(a, b, (((0,), (0,)), ((), ())), preferred_element_type=F32)


def _in_proj_kernel(x_ref, g_ref, w_ref, o_ref):
  n = _rmsnorm(x_ref[...], g_ref[...]).astype(BF16)
  for c in range(D_IN // PROJ_COL_CHUNK):
    cols = slice(c * PROJ_COL_CHUNK, (c + 1) * PROJ_COL_CHUNK)
    o_ref[:, cols] = _dot(n, w_ref[:, cols]).astype(BF16)


def _in_proj(x2d, g, w):
  t = x2d.shape[0]
  blocks = (_nbytes((ROW_TILE, D_MODEL), F32) + _nbytes((D_MODEL, D_IN), BF16)
            + _nbytes((ROW_TILE, D_IN), BF16))
  return pl.pallas_call(
      _in_proj_kernel,
      out_shape=jax.ShapeDtypeStruct((t, D_IN), BF16),
      grid=(t // ROW_TILE,),
      in_specs=[
          pl.BlockSpec((ROW_TILE, D_MODEL), lambda i: (i, 0)),
          pl.BlockSpec((1, D_MODEL), lambda i: (0, 0)),
          pl.BlockSpec((D_MODEL, D_IN), lambda i: (0, 0)),
      ],
      out_specs=pl.BlockSpec((ROW_TILE, D_IN), lambda i: (i, 0)),
      compiler_params=_compiler_params(("parallel",), blocks),
      name="in_proj",
  )(x2d, g, w)


def _rel_bucket(dist):
  max_exact = REL_BUCKETS // 2
  d = jnp.maximum(dist, 0)
  df = jnp.maximum(d, 1).astype(F32)
  large = max_exact + (jnp.log(df / max_exact) / math.log(REL_MAX_DIST / max_exact)
                       * (REL_BUCKETS - max_exact)).astype(jnp.int32)
  large = jnp.minimum(large, REL_BUCKETS - 1)
  return jnp.where(d < max_exact, d, large)


def _bias_kernel(tab_ref, bkt_ref, o_ref):
  h = pl.program_id(0)
  nb = o_ref.shape[1]
  bkt = bkt_ref[...]
  w = jnp.zeros(bkt.shape, F32)
  for b in range(REL_BUCKETS):
    w = jnp.where(bkt == b, tab_ref[b, h], w)
  c = lax.broadcasted_iota(jnp.int32, (MOBA_BLOCK, MOBA_BLOCK), 0)
  r = lax.broadcasted_iota(jnp.int32, (MOBA_BLOCK, MOBA_BLOCK), 1)
  for delta in range(nb):
    wide = jnp.broadcast_to(w[delta:delta + 1], (MOBA_BLOCK, 2 * MOBA_BLOCK))
    tile = pltpu.roll(wide, MOBA_BLOCK + 1, 1, stride=1, stride_axis=0)[:, :MOBA_BLOCK]
    tile = tile * LOG2E
    o_ref[0, delta] = jnp.where(r >= c, tile, NEG_INF) if delta == 0 else tile


def _bias_tiles(rel_bias, nb):
  m = jnp.arange(2 * MOBA_BLOCK, dtype=jnp.int32)
  dist = jnp.arange(nb, dtype=jnp.int32)[:, None] * MOBA_BLOCK + m[None, :] - (MOBA_BLOCK - 1)
  tiles = _nbytes((nb, MOBA_BLOCK, MOBA_BLOCK), F32)
  return pl.pallas_call(
      _bias_kernel,
      out_shape=jax.ShapeDtypeStruct((MOBA_HEADS, nb, MOBA_BLOCK, MOBA_BLOCK), F32),
      grid=(MOBA_HEADS,),
      in_specs=[
          pl.BlockSpec(memory_space=pltpu.SMEM),
          pl.BlockSpec((nb, 2 * MOBA_BLOCK), lambda h: (0, 0)),
      ],
      out_specs=pl.BlockSpec((1, nb, MOBA_BLOCK, MOBA_BLOCK), lambda h: (h, 0, 0, 0)),
      compiler_params=_compiler_params(("parallel",), tiles),
      name="bias_tiles",
  )(rel_bias, _rel_bucket(dist))


def _moba_kernel(q_ref, qall_ref, k_ref, v_ref, bias_ref, o_ref, pen_ref, s_ref, *, nb):
  i = pl.program_id(2)
  blk = MOBA_BLOCK
  dh = MOBA_HEAD_DIM
  heads = V7X_LANES // dh
  assert heads == 2, "each head borrows the other head's lanes for its probability sums"
  seq = nb * blk
  sub = V7X_SUBLANES

  def head_lanes(hh, shape):
    lane = lax.broadcasted_iota(jnp.int32, shape, len(shape) - 1)
    return (lane >= hh * dh) & (lane < (hh + 1) * dh)

  @pl.when(i == 0)
  def _():
    kmean = jnp.concatenate(
        [jnp.mean(k_ref[0, j * blk:(j + 1) * blk, :].astype(F32), axis=0, keepdims=True)
         for j in range(nb)], axis=0)
    q_all = qall_ref[0].astype(F32)
    jidx = lax.broadcasted_iota(jnp.int32, (nb, seq), 0)
    qblk = lax.broadcasted_iota(jnp.int32, (nb, seq), 1) // blk
    past = jidx < qblk
    for hh in range(heads):
      km = jnp.where(head_lanes(hh, kmean.shape), kmean, 0.0)
      gate = lax.dot_general(km, q_all, (((1,), (1,)), ((), ())),
                             precision=lax.Precision.HIGHEST, preferred_element_type=F32)
      gate = jnp.where(past, gate, NEG_INF)
      rank = jnp.zeros(gate.shape, F32)
      for jp in range(nb):
        row = gate[jp:jp + 1, :]
        ahead = jnp.where(row > gate, 1.0, jnp.where((row == gate) & (jidx > jp), 1.0, 0.0))
        rank = rank + ahead
      pen_ref[hh] = jnp.where(past, jnp.where(rank < MOBA_TOPK, 0.0, NEG_INF),
                              jnp.where(jidx == qblk, 0.0, NEG_INF))

  qt = MOBA_Q_BLOCKS * blk
  q = q_ref[0].astype(F32) * (dh ** -0.5)
  qz = [jnp.where(head_lanes(hh, q.shape), q, 0.0).astype(BF16) for hh in range(heads)]
  qcols = pl.ds(pl.multiple_of(i * qt, qt), qt)
  first_qblk = i * MOBA_Q_BLOCKS

  step = MOBA_STEP_BLOCKS * blk
  trips = (first_qblk + MOBA_Q_BLOCKS - 1 + MOBA_STEP_BLOCKS) // MOBA_STEP_BLOCKS

  def fold(x):
    return x.reshape(blk // sub, sub, qt)

  def score_pass(t, run_max):
    start = pl.multiple_of(t * step, step)
    k = k_ref[0, pl.ds(start, step), :]
    out = []
    for hh in range(heads):
      s = _dot_nt(k, qz[hh])
      mx = run_max[hh]
      for u in range(MOBA_STEP_BLOCKS):
        j = t * MOBA_STEP_BLOCKS + u
        bias = jnp.concatenate(
            [bias_ref[hh, jnp.maximum(first_qblk + qb - j, 0)] for qb in range(MOBA_Q_BLOCKS)],
            axis=1)
        part = s[u * blk:(u + 1) * blk] * LOG2E + bias
        s_ref[hh, pl.ds(start + u * blk, blk), :] = part
        mx = jnp.maximum(mx, jnp.max(fold(part), axis=0) + pen_ref[hh, pl.ds(j, 1), qcols])
      out.append(mx)
    return tuple(out)

  run_max = lax.fori_loop(0, trips, score_pass,
                          tuple(jnp.full((sub, qt), NEG_INF, F32) for _ in range(heads)))
  row_max = [jnp.max(mx, axis=0, keepdims=True) for mx in run_max]

  def value_pass(t, carry):
    start = pl.multiple_of(t * step, step)
    v = v_ref[0, pl.ds(start, step), :]
    out = []
    for hh in range(heads):
      lsum, acc = carry[hh]
      probs = []
      for u in range(MOBA_STEP_BLOCKS):
        j = t * MOBA_STEP_BLOCKS + u
        shift = row_max[hh] - pen_ref[hh, pl.ds(j, 1), qcols]
        p = jnp.exp2(s_ref[hh, pl.ds(start + u * blk, blk), :] - shift)
        probs.append(p.astype(BF16))
      v_ones = jnp.where(head_lanes(hh, v.shape), v, jnp.ones_like(v))
      pv = _dot_tn(v_ones, jnp.concatenate(probs, axis=0))
      other = (heads - 1 - hh) * dh
      out.append((lsum + pv[other:other + sub], acc + pv[hh * dh:(hh + 1) * dh]))
    return tuple(out)

  final = lax.fori_loop(
      0, trips, value_pass,
      tuple((jnp.zeros((sub, qt), F32), jnp.zeros((dh, qt), F32)) for _ in range(heads)))
  for hh, (lsum, acc) in enumerate(final):
    o_ref[0, hh * dh:(hh + 1) * dh, :] = (acc / lsum[0:1]).astype(o_ref.dtype)


def _moba(proj3, bias):
  b, s, _ = proj3.shape
  nb = s // MOBA_BLOCK
  assert nb % MOBA_STEP_BLOCKS == 0, "the inner loops may touch blocks past the own block"
  assert nb % MOBA_Q_BLOCKS == 0
  pairs = MOBA_W // V7X_LANES
  per_pair = V7X_LANES // MOBA_HEAD_DIM
  qt = MOBA_Q_BLOCKS * MOBA_BLOCK
  blocks = (2 * _nbytes((qt, V7X_LANES), BF16) + 3 * _nbytes((s, V7X_LANES), BF16)
            + _nbytes((per_pair, nb, MOBA_BLOCK, MOBA_BLOCK), F32))
  scratch = _nbytes((per_pair, nb, s), F32) + _nbytes((per_pair, s, qt), F32)
  seq_blk = lambda off: pl.BlockSpec((1, s, V7X_LANES), lambda bi, hp, i: (bi, 0, off + hp))
  return pl.pallas_call(
      functools.partial(_moba_kernel, nb=nb),
      out_shape=jax.ShapeDtypeStruct((b, MOBA_W, s), BF16),
      grid=(b, pairs, nb // MOBA_Q_BLOCKS),
      in_specs=[
          pl.BlockSpec((1, qt, V7X_LANES), lambda bi, hp, i: (bi, i, _QA_BLK + hp)),
          seq_blk(_QA_BLK), seq_blk(_KA_BLK), seq_blk(_VA_BLK),
          pl.BlockSpec((per_pair, nb, MOBA_BLOCK, MOBA_BLOCK), lambda bi, hp, i: (hp, 0, 0, 0)),
      ],
      out_specs=pl.BlockSpec((1, V7X_LANES, qt), lambda bi, hp, i: (bi, hp, i)),
      scratch_shapes=[
          pltpu.VMEM((per_pair, nb, s), F32),
          pltpu.VMEM((per_pair, s, qt), F32),
      ],
      compiler_params=_compiler_params(("parallel", "parallel", "arbitrary"), blocks, scratch),
      name="moba",
  )(proj3, proj3, proj3, proj3, bias)


def _retention_kernel(q_ref, k_ref, v_ref, g_ref, cos_ref, sin_ref, decay_ref, qin_ref, kout_ref,
                      cdec_ref, gain_ref, o_ref, state_ref):
  c_len = RET_CHUNK
  half = RET_DIM // 2

  @pl.when(pl.program_id(1) == 0)
  def _():
    state_ref[...] = jnp.zeros(state_ref.shape, F32)

  for c in range(RET_ROWS // c_len):
    rows = slice(c * c_len, (c + 1) * c_len)
    cos = cos_ref[rows, :]
    sin = sin_ref[rows, :]
    for h in range(RET_HEADS):
      lanes = slice(h * RET_DIM, (h + 1) * RET_DIM)
      q = q_ref[0, rows, lanes].astype(F32)
      k = k_ref[0, rows, lanes].astype(F32)
      v = v_ref[0, rows, lanes]
      qr = q * cos + pltpu.roll(q, half, 1) * sin
      kr = (k * cos + pltpu.roll(k, half, 1) * sin) * (RET_DIM ** -0.5)
      state = state_ref[h]
      sc = _dot_nt(qr.astype(BF16), kr.astype(BF16)) * decay_ref[h]
      inner = _dot(sc.astype(BF16), v)
      cross = _dot((qr * qin_ref[h]).astype(BF16), state.astype(BF16))
      kv = _dot_tn((kr * kout_ref[h]).astype(BF16), v)
      state_ref[h] = cdec_ref[h] * state + kv
      y = inner + cross
      mu = jnp.mean(y, axis=-1, keepdims=True)
      yc = y - mu
      var = jnp.mean(yc * yc, axis=-1, keepdims=True)
      yn = yc * lax.rsqrt(var + EPS) * gain_ref[:, lanes]
      g = g_ref[0, rows, lanes].astype(F32)
      o_ref[0, rows, lanes] = (yn * (g * jax.nn.sigmoid(g))).astype(o_ref.dtype)


def _retention(proj3, gn_gain):
  b, s, _ = proj3.shape
  c_len = RET_CHUNK
  half = RET_DIM // 2
  pos = jnp.arange(s, dtype=F32)
  inv = ROPE_BASE ** (-jnp.arange(half, dtype=F32) / half)
  ang = pos[:, None] * inv
  cos = jnp.concatenate([jnp.cos(ang), jnp.cos(ang)], axis=-1)
  sin = jnp.concatenate([-jnp.sin(ang), jnp.sin(ang)], axis=-1)
  log_gamma = jnp.log1p(-jnp.power(2.0, -5.0 - jnp.arange(RET_HEADS, dtype=F32)))
  idx = jnp.arange(c_len, dtype=F32)
  diff = idx[:, None] - idx[None, :]
  decay = jnp.where(diff >= 0, jnp.exp(log_gamma[:, None, None] * jnp.maximum(diff, 0.0)), 0.0)
  lanes = (RET_HEADS, c_len, RET_DIM)
  q_in = jnp.broadcast_to(jnp.exp(log_gamma[:, None] * (idx + 1.0))[:, :, None], lanes)
  k_out = jnp.broadcast_to(jnp.exp(log_gamma[:, None] * (c_len - 1.0 - idx))[:, :, None], lanes)
  cdec = jnp.broadcast_to(jnp.exp(log_gamma * c_len)[:, None, None], (RET_HEADS, 1, RET_DIM))

  first = _QR_BLK * V7X_LANES // RET_W
  row_blk = lambda off: pl.BlockSpec((1, RET_ROWS, RET_W), lambda bi, t: (bi, t, first + off))
  head_blk = lambda shape: pl.BlockSpec((RET_HEADS,) + shape, lambda bi, t: (0, 0, 0))
  blocks = (5 * _nbytes((RET_ROWS, RET_W), BF16) + 2 * _nbytes((RET_ROWS, RET_DIM), F32)
            + 3 * _nbytes((RET_HEADS, c_len, RET_DIM), F32))
  return pl.pallas_call(
      _retention_kernel,
      out_shape=jax.ShapeDtypeStruct((b, s, RET_W), BF16),
      grid=(b, s // RET_ROWS),
      in_specs=[
          row_blk(0), row_blk(1), row_blk(2), row_blk(3),
          pl.BlockSpec((RET_ROWS, RET_DIM), lambda bi, t: (t, 0)),
          pl.BlockSpec((RET_ROWS, RET_DIM), lambda bi, t: (t, 0)),
          head_blk((c_len, c_len)), head_blk((c_len, RET_DIM)), head_blk((c_len, RET_DIM)),
          head_blk((1, RET_DIM)),
          pl.BlockSpec((1, RET_W), lambda bi, t: (0, 0)),
      ],
      out_specs=pl.BlockSpec((1, RET_ROWS, RET_W), lambda bi, t: (bi, t, 0)),
      scratch_shapes=[pltpu.VMEM((RET_HEADS, RET_DIM, RET_DIM), F32)],
      compiler_params=_compiler_params(("parallel", "arbitrary"), blocks,
                                       _nbytes((RET_HEADS, RET_DIM, RET_DIM), F32)),
      name="retention",
  )(proj3, proj3, proj3, proj3, cos, sin, decay, q_in, k_out, cdec, gn_gain)


def _mem_kv_kernel(m_ref, g_ref, w_ref, o_ref):
  n = _rmsnorm(m_ref[0], g_ref[...]).astype(BF16)
  o_ref[0] = _dot(n, w_ref[...]).astype(BF16)


def _mem_kv(mem, g, w):
  b = mem.shape[0]
  blocks = (_nbytes((MEM_LEN, D_MODEL), F32) + _nbytes((D_MODEL, 2 * MEM_W), BF16)
            + _nbytes((MEM_LEN, 2 * MEM_W), BF16))
  return pl.pallas_call(
      _mem_kv_kernel,
      out_shape=jax.ShapeDtypeStruct((b, MEM_LEN, 2 * MEM_W), BF16),
      grid=(b,),
      in_specs=[
          pl.BlockSpec((1, MEM_LEN, D_MODEL), lambda i: (i, 0, 0)),
          pl.BlockSpec((1, D_MODEL), lambda i: (0, 0)),
          pl.BlockSpec((D_MODEL, 2 * MEM_W), lambda i: (0, 0)),
      ],
      out_specs=pl.BlockSpec((1, MEM_LEN, 2 * MEM_W), lambda i: (i, 0, 0)),
      compiler_params=_compiler_params(("parallel",), blocks),
      name="mem_kv",
  )(mem, g, w)


def _mem_attn_kernel(q_ref, kv_ref, o_ref):
  dh = MEM_HEAD_DIM
  outs = []
  for h in range(MEM_HEADS):
    q = q_ref[0, :, h * dh:(h + 1) * dh]
    k = kv_ref[0, :, h * dh:(h + 1) * dh]
    v = kv_ref[0, :, MEM_W + h * dh:MEM_W + (h + 1) * dh]
    s = _dot_nt(q, k) * (dh ** -0.5)
    p = jnp.exp(s - jnp.max(s, axis=-1, keepdims=True))
    l = jnp.sum(p, axis=-1, keepdims=True)
    outs.append(_dot(p.astype(BF16), v) / l)
  o_ref[0] = jnp.concatenate(outs, axis=1).astype(o_ref.dtype)


def _mem_attn(proj3, kv):
  b, s, _ = proj3.shape
  blocks = (2 * _nbytes((ROW_TILE, MEM_W), BF16) + _nbytes((MEM_LEN, 2 * MEM_W), BF16))
  return pl.pallas_call(
      _mem_attn_kernel,
      out_shape=jax.ShapeDtypeStruct((b, s, MEM_W), BF16),
      grid=(b, s // ROW_TILE),
      in_specs=[
          pl.BlockSpec((1, ROW_TILE, MEM_W), lambda bi, t: (bi, t, _QM_COL // MEM_W)),
          pl.BlockSpec((1, MEM_LEN, 2 * MEM_W), lambda bi, t: (bi, 0, 0)),
      ],
      out_specs=pl.BlockSpec((1, ROW_TILE, MEM_W), lambda bi, t: (bi, t, 0)),
      compiler_params=_compiler_params(("parallel", "parallel"), blocks),
      name="mem_attn",
  )(proj3, kv)


def _merge_kernel(x_ref, ya_ref, yr_ref, ym_ref, za_ref, zr_ref, zm_ref, wa_ref, wr_ref, wm_ref,
                  wo_ref, o_ref):
  def gate(z_ref, proj):
    return jax.nn.sigmoid(z_ref[...].astype(F32)) * proj

  merged = (gate(za_ref, _dot_tn(ya_ref[0], wa_ref[...]))
            + gate(zr_ref, _dot(yr_ref[...], wr_ref[...]))
            + gate(zm_ref, _dot(ym_ref[...], wm_ref[...])))
  o_ref[...] = x_ref[...] + _dot(merged.astype(BF16), wo_ref[...])


def _merge(x2d, ya_t, yr, ym, proj2d, wa, wr, wm, wo):
  t = x2d.shape[0]
  tiles_per_seq = ya_t.shape[2] // ROW_TILE
  zblk = _Z_COL // D_MODEL
  row = lambda w, col=0: pl.BlockSpec((ROW_TILE, w), lambda i: (i, col))
  ya_spec = pl.BlockSpec((1, MOBA_W, ROW_TILE),
                         lambda i: (i // tiles_per_seq, 0, i % tiles_per_seq))
  full = lambda shape: pl.BlockSpec(shape, lambda i: (0, 0))
  blocks = (2 * _nbytes((ROW_TILE, D_MODEL), F32) + 3 * _nbytes((ROW_TILE, MOBA_W), BF16)
            + 3 * _nbytes((ROW_TILE, D_MODEL), BF16) + 3 * _nbytes((MOBA_W, D_MODEL), BF16)
            + _nbytes((D_MODEL, D_MODEL), BF16))
  return pl.pallas_call(
      _merge_kernel,
      out_shape=jax.ShapeDtypeStruct((t, D_MODEL), F32),
      grid=(t // ROW_TILE,),
      in_specs=[
          row(D_MODEL), ya_spec, row(RET_W), row(MEM_W),
          row(D_MODEL, zblk), row(D_MODEL, zblk + 1), row(D_MODEL, zblk + 2),
          full((MOBA_W, D_MODEL)), full((RET_W, D_MODEL)), full((MEM_W, D_MODEL)),
          full((D_MODEL, D_MODEL)),
      ],
      out_specs=row(D_MODEL),
      compiler_params=_compiler_params(("parallel",), blocks),
      name="merge",
  )(x2d, ya_t, yr, ym, proj2d, proj2d, proj2d, wa, wr, wm, wo)


def _ffn_kernel(h_ref, gf_ref, wup_ref, cw_ref, cb_ref, wdn_ref, gfin_ref, o_ref, hid_ref,
                act_ref):
  tm = ROW_TILE
  halo = CONV_HALO
  cw = FFN_COL_CHUNK
  n_chunks = D_FF // cw

  @pl.when(pl.program_id(1) == 0)
  def _():
    hid_ref[:, 0:halo, :] = jnp.zeros((2 * n_chunks, halo, cw), F32)

  h1 = h_ref[0]
  n = _rmsnorm(h1, gf_ref[...]).astype(BF16)

  for c in range(n_chunks):
    for part in range(2):
      col = part * D_FF + c * cw
      hid_ref[2 * c + part, halo:halo + tm, :] = _dot(n, wup_ref[:, col:col + cw])

  top_row = lax.broadcasted_iota(jnp.int32, (halo, cw), 0) == 0

  def shift_down(v, top):
    rolled = pltpu.roll(v, 1, 0)
    return jnp.concatenate([jnp.where(top_row, top, rolled[0:halo]), rolled[halo:]], axis=0)

  assert CONV_WIDTH == 3
  acc = None
  done = 0
  for c in range(n_chunks):
    halves = []
    for part in range(2):
      col = part * D_FF + c * cw
      cols = slice(col, col + cw)
      buf = hid_ref.at[2 * c + part]
      x = buf[halo:halo + tm, :]
      last = buf[halo - 1:halo, :]
      before_last = buf[halo - 2:halo - 1, :]
      w0, w1, w2 = (cw_ref[tap:tap + 1, cols] for tap in range(CONV_WIDTH))
      a = shift_down(w0 * x, w0 * last)
      b = shift_down(w1 * x + a, w1 * last + w0 * before_last)
      halves.append((cb_ref[:, cols] + w2 * x) + b)
      buf[0:halo, :] = buf[tm:tm + halo, :]
    gate, up = halves
    act = gate * (1.0 + lax.erf(gate * math.sqrt(0.5))) * up
    act_ref[:, c * cw:(c + 1) * cw] = act.astype(BF16)
    if c + 1 in (FFN_DOWN_SPLIT, n_chunks):
      ks = slice(done * cw, (c + 1) * cw)
      part_sum = _dot(act_ref[:, ks], wdn_ref[ks, :])
      acc = part_sum if acc is None else acc + part_sum
      done = c + 1

  o_ref[0] = _rmsnorm(h1 + acc, gfin_ref[...])


def _ffn(h3, g_ffn, w_up, conv_w, conv_b, w_down, g_final):
  b, s, _ = h3.shape
  full = lambda shape: pl.BlockSpec(shape, lambda bi, t: (0, 0))
  once = lambda shape: pl.BlockSpec(shape, lambda bi, t: (0, 0), pipeline_mode=pl.Buffered(1))
  tiles = 2 * _nbytes((ROW_TILE, D_MODEL), F32) + _nbytes((CONV_WIDTH + 1, 2 * D_FF), F32)
  weights = _nbytes((D_MODEL, 2 * D_FF), BF16) + _nbytes((D_FF, D_MODEL), BF16)
  slots = 2 * D_FF // FFN_COL_CHUNK
  scratch = (_nbytes((slots, ROW_TILE + CONV_HALO, FFN_COL_CHUNK), F32)
             + _nbytes((ROW_TILE, D_FF), BF16))
  return pl.pallas_call(
      _ffn_kernel,
      out_shape=jax.ShapeDtypeStruct((b, s, D_MODEL), F32),
      grid=(b, s // ROW_TILE),
      in_specs=[
          pl.BlockSpec((1, ROW_TILE, D_MODEL), lambda bi, t: (bi, t, 0)),
          full((1, D_MODEL)),
          once((D_MODEL, 2 * D_FF)),
          full((CONV_WIDTH, 2 * D_FF)),
          full((1, 2 * D_FF)),
          once((D_FF, D_MODEL)),
          full((1, D_MODEL)),
      ],
      out_specs=pl.BlockSpec((1, ROW_TILE, D_MODEL), lambda bi, t: (bi, t, 0)),
      scratch_shapes=[
          pltpu.VMEM((slots, ROW_TILE + CONV_HALO, FFN_COL_CHUNK), F32),
          pltpu.VMEM((ROW_TILE, D_FF), BF16),
      ],
      compiler_params=_compiler_params(("parallel", "arbitrary"), tiles, weights + scratch),
      name="ffn",
  )(h3, g_ffn, w_up, conv_w, conv_b, w_down, g_final)


def kernel(x, mem, g_mix, w_in, rel_bias, ret_gn_gain, g_mem, w_mem_kv, w_br_attn, w_br_ret,
           w_br_mem, w_out, g_ffn, w_up, conv_w, conv_b, w_down, g_final):
  b, s, d = x.shape
  assert d == D_MODEL and mem.shape == (b, MEM_LEN, D_MODEL)
  assert s % MOBA_BLOCK == 0 and s % ROW_TILE == 0 and s % RET_ROWS == 0
  assert g_mix.shape[0] == 1, "single-layer block"
  h = x.reshape(b * s, d)
  bias = _bias_tiles(rel_bias, s // MOBA_BLOCK)
  proj = _in_proj(h, g_mix, w_in[0].astype(BF16))
  proj3 = proj.reshape(b, s, D_IN)
  ya_t = _moba(proj3, bias)
  y_r = _retention(proj3, ret_gn_gain)
  kv = _mem_kv(mem, g_mem, w_mem_kv[0].astype(BF16))
  y_m = _mem_attn(proj3, kv)
  h1 = _merge(h, ya_t, y_r.reshape(b * s, RET_W), y_m.reshape(b * s, MEM_W),
              proj, w_br_attn[0].astype(BF16), w_br_ret[0].astype(BF16),
              w_br_mem[0].astype(BF16), w_out[0].astype(BF16))
  return _ffn(h1.reshape(b, s, d), g_ffn, w_up[0].astype(BF16), conv_w[0], conv_b,
              (0.5 * w_down[0]).astype(BF16), g_final.reshape(1, d))
```

```python
import functools
import math

import jax
import jax.numpy as jnp
from jax import lax
from jax.experimental import pallas as pl
from jax.experimental.pallas import tpu as pltpu

F32 = jnp.float32
BF16 = jnp.bfloat16

D_MODEL = 1024
MEM_LEN = 256
MOBA_HEADS = 8
MOBA_HEAD_DIM = 64
MOBA_BLOCK = 256
MOBA_TOPK = 3
RET_HEADS = 4
RET_DIM = 128
RET_CHUNK = 128
ROPE_BASE = 10000.0
MEM_HEADS = 4
MEM_HEAD_DIM = 128
REL_BUCKETS = 32
REL_MAX_DIST = 2048
D_FF = 2816
CONV_WIDTH = 3
EPS = 1e-6
NEG_INF = -1e30
LOG2E = math.log2(math.e)
F32_LOWEST = float(jnp.finfo(jnp.float32).min)
F32_AS_BF16_PIECES = 3

MOBA_W = MOBA_HEADS * MOBA_HEAD_DIM
RET_W = RET_HEADS * RET_DIM
MEM_W = MEM_HEADS * MEM_HEAD_DIM
D_IN = 3 * MOBA_W + 4 * RET_W + MEM_W + 3 * D_MODEL

V7X_LANES = 128
V7X_SUBLANES = 8
V7X_VMEM_BYTES = 64 * 1024 * 1024
V7X_VMEM_RESERVE_BYTES = 6 * 1024 * 1024

_QA_BLK = 0
_KA_BLK = MOBA_W // V7X_LANES
_VA_BLK = 2 * MOBA_W // V7X_LANES
_QR_BLK = 3 * MOBA_W // V7X_LANES
_KR_BLK = _QR_BLK + RET_W // V7X_LANES
_VR_BLK = _KR_BLK + RET_W // V7X_LANES
_GR_BLK = _VR_BLK + RET_W // V7X_LANES
_QM_COL = 3 * MOBA_W + 4 * RET_W
_Z_COL = _QM_COL + MEM_W

ROW_TILE = 512
PROJ_COL_CHUNK = 512
RET_ROWS = 512
FFN_COL_CHUNK = 256
FFN_DOWN_SPLIT = 6
MOBA_STEP_BLOCKS = 4
MOBA_Q_BLOCKS = 2
CONV_HALO = V7X_SUBLANES


def _compiler_params(semantics, block_bytes, scratch_bytes=0, flags=None):
  need = 2 * block_bytes + scratch_bytes + 12 * 1024 * 1024
  limit = min(need, V7X_VMEM_BYTES - V7X_VMEM_RESERVE_BYTES)
  return pltpu.CompilerParams(dimension_semantics=semantics, vmem_limit_bytes=limit, flags=flags)


def _nbytes(shape, dtype):
  return math.prod(shape) * jnp.dtype(dtype).itemsize


def _rmsnorm(x, g):
  ms = jnp.mean(x * x, axis=-1, keepdims=True)
  return x * lax.rsqrt(ms + EPS) * g


def _dot(a, b):
  return jnp.dot(a, b, preferred_element_type=F32)


def _dot_nt(a, b):
  return lax.dot_general(a, b, (((1,), (1,)), ((), ())), preferred_element_type=F32)


def _dot_tn(a, b):
  return lax.dot_general(a, b, (((0,), (0,)), ((), ())), preferred_element_type=F32)


def _in_proj_kernel(x_ref, g_ref, w_ref, o_ref):
  n = _rmsnorm(x_ref[...], g_ref[...]).astype(BF16)
  for c in range(D_IN // PROJ_COL_CHUNK):
    cols = slice(c * PROJ_COL_CHUNK, (c + 1) * PROJ_COL_CHUNK)
    o_ref[:, cols] = _dot(n, w_ref[:, cols]).astype(BF16)


def _in_proj(x2d, g, w):
  t = x2d.shape[0]
  blocks = (_nbytes((ROW_TILE, D_MODEL), F32) + _nbytes((D_MODEL, D_IN), BF16)
            + _nbytes((ROW_TILE, D_IN), BF16))
  return pl.pallas_call(
      _in_proj_kernel,
      out_shape=jax.ShapeDtypeStruct((t, D_IN), BF16),
      grid=(t // ROW_TILE,),
      in_specs=[
          pl.BlockSpec((ROW_TILE, D_MODEL), lambda i: (i, 0)),
          pl.BlockSpec((1, D_MODEL), lambda i: (0, 0)),
          pl.BlockSpec((D_MODEL, D_IN), lambda i: (0, 0)),
      ],
      out_specs=pl.BlockSpec((ROW_TILE, D_IN), lambda i: (i, 0)),
      compiler_params=_compiler_params(("parallel",), blocks),
      name="in_proj",
  )(x2d, g, w)


def _rel_bucket(dist):
  max_exact = REL_BUCKETS // 2
  d = jnp.maximum(dist, 0)
  df = jnp.maximum(d, 1).astype(F32)
  large = max_exact + (jnp.log(df / max_exact) / math.log(REL_MAX_DIST / max_exact)
                       * (REL_BUCKETS - max_exact)).astype(jnp.int32)
  large = jnp.minimum(large, REL_BUCKETS - 1)
  return jnp.where(d < max_exact, d, large)


def _bias_kernel(tab_ref, bkt_ref, o_ref):
  h = pl.program_id(0)
  nb = o_ref.shape[1]
  bkt = bkt_ref[...]
  w = jnp.zeros(bkt.shape, F32)
  for b in range(REL_BUCKETS):
    w = jnp.where(bkt == b, tab_ref[b, h], w)
  c = lax.broadcasted_iota(jnp.int32, (MOBA_BLOCK, MOBA_BLOCK), 0)
  r = lax.broadcasted_iota(jnp.int32, (MOBA_BLOCK, MOBA_BLOCK), 1)
  for delta in range(nb):
    wide = jnp.broadcast_to(w[delta:delta + 1], (MOBA_BLOCK, 2 * MOBA_BLOCK))
    tile = pltpu.roll(wide, MOBA_BLOCK + 1, 1, stride=1, stride_axis=0)[:, :MOBA_BLOCK]
    tile = tile * LOG2E
    o_ref[0, delta] = jnp.where(r >= c, tile, NEG_INF) if delta == 0 else tile


def _bias_tiles(rel_bias, nb):
  m = jnp.arange(2 * MOBA_BLOCK, dtype=jnp.int32)
  dist = jnp.arange(nb, dtype=jnp.int32)[:, None] * MOBA_BLOCK + m[None, :] - (MOBA_BLOCK - 1)
  tiles = _nbytes((nb, MOBA_BLOCK, MOBA_BLOCK), F32)
  return pl.pallas_call(
      _bias_kernel,
      out_shape=jax.ShapeDtypeStruct((MOBA_HEADS, nb, MOBA_BLOCK, MOBA_BLOCK), F32),
      grid=(MOBA_HEADS,),
      in_specs=[
          pl.BlockSpec(memory_space=pltpu.SMEM),
          pl.BlockSpec((nb, 2 * MOBA_BLOCK), lambda h: (0, 0)),
      ],
      out_specs=pl.BlockSpec((1, nb, MOBA_BLOCK, MOBA_BLOCK), lambda h: (h, 0, 0, 0)),
      compiler_params=_compiler_params(("parallel",), tiles),
      name="bias_tiles",
  )(rel_bias, _rel_bucket(dist))


def _moba_kernel(q_ref, qall_ref, k_ref, v_ref, bias_ref, o_ref, pen_ref, s_ref, *, nb):
  i = pl.program_id(2)
  blk = MOBA_BLOCK
  dh = MOBA_HEAD_DIM
  heads = V7X_LANES // dh
  assert heads == 2, "each head borrows the other head's lanes for its probability sums"
  seq = nb * blk
  sub = V7X_SUBLANES

  def head_lanes(hh, shape):
    lane = lax.broadcasted_iota(jnp.int32, shape, len(shape) - 1)
    return (lane >= hh * dh) & (lane < (hh + 1) * dh)

  @pl.when(i == 0)
  def _():
    kmean = jnp.concatenate(
        [jnp.mean(k_ref[0, j * blk:(j + 1) * blk, :].astype(F32), axis=0, keepdims=True)
         for j in range(nb)], axis=0)
    pieces, rest = [], kmean
    for _ in range(F32_AS_BF16_PIECES):
      piece = rest.astype(BF16)
      pieces.append(piece)
      rest = rest - piece.astype(F32)
    zero = jnp.zeros_like(pieces[0])
    stacked = jnp.concatenate([jnp.where(head_lanes(hh, piece.shape), piece, zero)
                               for hh in range(heads) for piece in pieces], axis=0)
    gates = _dot_nt(stacked, qall_ref[0])
    jidx = lax.broadcasted_iota(jnp.int32, (nb, seq), 0)
    qblk = lax.broadcasted_iota(jnp.int32, (nb, seq), 1) // blk
    past = jidx < qblk
    for hh in range(heads):
      base = hh * F32_AS_BF16_PIECES * nb
      gate = sum(gates[base + p * nb:base + (p + 1) * nb] for p in range(F32_AS_BF16_PIECES))
      gate = jnp.where(past, gate, NEG_INF)
      taken = jnp.zeros(gate.shape, F32)
      for _ in range(MOBA_TOPK):
        best = jnp.max(gate, axis=0, keepdims=True)
        first = jnp.min(jnp.where(gate == best, jidx, nb), axis=0, keepdims=True)
        hit = jidx == first
        taken = jnp.where(hit, 1.0, taken)
        gate = jnp.where(hit, F32_LOWEST, gate)
      pen_ref[hh] = jnp.where(past, jnp.where(taken > 0.0, 0.0, NEG_INF),
                              jnp.where(jidx == qblk, 0.0, NEG_INF))

  qt = MOBA_Q_BLOCKS * blk
  q = q_ref[0].astype(F32) * (dh ** -0.5)
  qz = [jnp.where(head_lanes(hh, q.shape), q, 0.0).astype(BF16) for hh in range(heads)]
  qcols = pl.ds(pl.multiple_of(i * qt, qt), qt)
  first_qblk = i * MOBA_Q_BLOCKS

  step = MOBA_STEP_BLOCKS * blk
  trips = (first_qblk + MOBA_Q_BLOCKS - 1 + MOBA_STEP_BLOCKS) // MOBA_STEP_BLOCKS

  def fold(x):
    return x.reshape(blk // sub, sub, qt)

  def score_pass(t, run_max):
    start = pl.multiple_of(t * step, step)
    k = k_ref[0, pl.ds(start, step), :]
    out = []
    for hh in range(heads):
      s = _dot_nt(k, qz[hh])
      mx = run_max[hh]
      for u in range(MOBA_STEP_BLOCKS):
        j = t * MOBA_STEP_BLOCKS + u
        bias = jnp.concatenate(
            [bias_ref[hh, jnp.maximum(first_qblk + qb - j, 0)] for qb in range(MOBA_Q_BLOCKS)],
            axis=1)
        part = s[u * blk:(u + 1) * blk] * LOG2E + bias
        s_ref[hh, pl.ds(start + u * blk, blk), :] = part
        mx = jnp.maximum(mx, jnp.max(fold(part), axis=0) + pen_ref[hh, pl.ds(j, 1), qcols])
      out.append(mx)
    return tuple(out)

  run_max = lax.fori_loop(0, trips, score_pass,
                          tuple(jnp.full((sub, qt), NEG_INF, F32) for _ in range(heads)))
  row_max = [jnp.max(mx, axis=0, keepdims=True) for mx in run_max]

  def value_pass(t, carry):
    start = pl.multiple_of(t * step, step)
    v = v_ref[0, pl.ds(start, step), :]
    out = []
    for hh in range(heads):
      lsum, acc = carry[hh]
      probs = []
      for u in range(MOBA_STEP_BLOCKS):
        j = t * MOBA_STEP_BLOCKS + u
        shift = row_max[hh] - pen_ref[hh, pl.ds(j, 1), qcols]
        p = jnp.exp2(s_ref[hh, pl.ds(start + u * blk, blk), :] - shift)
        probs.append(p.astype(BF16))
      v_ones = jnp.where(head_lanes(hh, v.shape), v, jnp.ones_like(v))
      pv = _dot_tn(v_ones, jnp.concatenate(probs, axis=0))
      other = (heads - 1 - hh) * dh
      out.append((lsum + pv[other:other + sub], acc + pv[hh * dh:(hh + 1) * dh]))
    return tuple(out)

  final = lax.fori_loop(
      0, trips, value_pass,
      tuple((jnp.zeros((sub, qt), F32), jnp.zeros((dh, qt), F32)) for _ in range(heads)))
  for hh, (lsum, acc) in enumerate(final):
    o_ref[0, hh * dh:(hh + 1) * dh, :] = (acc / lsum[0:1]).astype(o_ref.dtype)


def _moba(proj3, bias):
  b, s, _ = proj3.shape
  nb = s // MOBA_BLOCK
  assert nb % MOBA_STEP_BLOCKS == 0, "the inner loops may touch blocks past the own block"
  assert nb % MOBA_Q_BLOCKS == 0
  pairs = MOBA_W // V7X_LANES
  per_pair = V7X_LANES // MOBA_HEAD_DIM
  qt = MOBA_Q_BLOCKS * MOBA_BLOCK
  blocks = (2 * _nbytes((qt, V7X_LANES), BF16) + 3 * _nbytes((s, V7X_LANES), BF16)
            + _nbytes((per_pair, nb, MOBA_BLOCK, MOBA_BLOCK), F32))
  scratch = _nbytes((per_pair, nb, s), F32) + _nbytes((per_pair, s, qt), F32)
  seq_blk = lambda off: pl.BlockSpec((1, s, V7X_LANES), lambda hp, bi, i: (bi, 0, off + hp))
  return pl.pallas_call(
      functools.partial(_moba_kernel, nb=nb),
      out_shape=jax.ShapeDtypeStruct((b, MOBA_W, s), BF16),
      grid=(pairs, b, nb // MOBA_Q_BLOCKS),
      in_specs=[
          pl.BlockSpec((1, qt, V7X_LANES), lambda hp, bi, i: (bi, i, _QA_BLK + hp)),
          seq_blk(_QA_BLK), seq_blk(_KA_BLK), seq_blk(_VA_BLK),
          pl.BlockSpec((per_pair, nb, MOBA_BLOCK, MOBA_BLOCK), lambda hp, bi, i: (hp, 0, 0, 0)),
      ],
      out_specs=pl.BlockSpec((1, V7X_LANES, qt), lambda hp, bi, i: (bi, hp, i)),
      scratch_shapes=[
          pltpu.VMEM((per_pair, nb, s), F32),
          pltpu.VMEM((per_pair, s, qt), F32),
      ],
      compiler_params=_compiler_params(("parallel", "parallel", "arbitrary"), blocks, scratch),
      name="moba",
  )(proj3, proj3, proj3, proj3, bias)


def _retention_kernel(q_ref, k_ref, v_ref, g_ref, cos_ref, sin_ref, decay_ref, qin_ref, kout_ref,
                      cdec_ref, gain_ref, o_ref, state_ref):
  c_len = RET_CHUNK
  half = RET_DIM // 2

  @pl.when(pl.program_id(1) == 0)
  def _():
    state_ref[...] = jnp.zeros(state_ref.shape, F32)

  for c in range(RET_ROWS // c_len):
    rows = slice(c * c_len, (c + 1) * c_len)
    cos = cos_ref[rows, :]
    sin = sin_ref[rows, :]
    for h in range(RET_HEADS):
      lanes = slice(h * RET_DIM, (h + 1) * RET_DIM)
      q = q_ref[0, rows, lanes].astype(F32)
      k = k_ref[0, rows, lanes].astype(F32)
      v = v_ref[0, rows, lanes]
      qr = q * cos + pltpu.roll(q, half, 1) * sin
      kr = (k * cos + pltpu.roll(k, half, 1) * sin) * (RET_DIM ** -0.5)
      state = state_ref[h]
      sc = _dot_nt(qr.astype(BF16), kr.astype(BF16)) * decay_ref[h]
      inner = _dot(sc.astype(BF16), v)
      cross = _dot((qr * qin_ref[h]).astype(BF16), state.astype(BF16))
      kv = _dot_tn((kr * kout_ref[h]).astype(BF16), v)
      state_ref[h] = cdec_ref[h] * state + kv
      y = inner + cross
      mu = jnp.mean(y, axis=-1, keepdims=True)
      yc = y - mu
      var = jnp.mean(yc * yc, axis=-1, keepdims=True)
      yn = yc * lax.rsqrt(var + EPS) * gain_ref[:, lanes]
      g = g_ref[0, rows, lanes].astype(F32)
      o_ref[0, rows, lanes] = (yn * (g * jax.nn.sigmoid(g))).astype(o_ref.dtype)


def _retention(proj3, gn_gain):
  b, s, _ = proj3.shape
  c_len = RET_CHUNK
  half = RET_DIM // 2
  pos = jnp.arange(s, dtype=F32)
  inv = ROPE_BASE ** (-jnp.arange(half, dtype=F32) / half)
  ang = pos[:, None] * inv
  cos = jnp.concatenate([jnp.cos(ang), jnp.cos(ang)], axis=-1)
  sin = jnp.concatenate([-jnp.sin(ang), jnp.sin(ang)], axis=-1)
  log_gamma = jnp.log1p(-jnp.power(2.0, -5.0 - jnp.arange(RET_HEADS, dtype=F32)))
  idx = jnp.arange(c_len, dtype=F32)
  diff = idx[:, None] - idx[None, :]
  decay = jnp.where(diff >= 0, jnp.exp(log_gamma[:, None, None] * jnp.maximum(diff, 0.0)), 0.0)
  lanes = (RET_HEADS, c_len, RET_DIM)
  q_in = jnp.broadcast_to(jnp.exp(log_gamma[:, None] * (idx + 1.0))[:, :, None], lanes)
  k_out = jnp.broadcast_to(jnp.exp(log_gamma[:, None] * (c_len - 1.0 - idx))[:, :, None], lanes)
  cdec = jnp.broadcast_to(jnp.exp(log_gamma * c_len)[:, None, None], (RET_HEADS, 1, RET_DIM))

  first = _QR_BLK * V7X_LANES // RET_W
  row_blk = lambda off: pl.BlockSpec((1, RET_ROWS, RET_W), lambda bi, t: (bi, t, first + off))
  head_blk = lambda shape: pl.BlockSpec((RET_HEADS,) + shape, lambda bi, t: (0, 0, 0))
  blocks = (5 * _nbytes((RET_ROWS, RET_W), BF16) + 2 * _nbytes((RET_ROWS, RET_DIM), F32)
            + 3 * _nbytes((RET_HEADS, c_len, RET_DIM), F32))
  return pl.pallas_call(
      _retention_kernel,
      out_shape=jax.ShapeDtypeStruct((b, s, RET_W), BF16),
      grid=(b, s // RET_ROWS),
      in_specs=[
          row_blk(0), row_blk(1), row_blk(2), row_blk(3),
          pl.BlockSpec((RET_ROWS, RET_DIM), lambda bi, t: (t, 0)),
          pl.BlockSpec((RET_ROWS, RET_DIM), lambda bi, t: (t, 0)),
          head_blk((c_len, c_len)), head_blk((c_len, RET_DIM)), head_blk((c_len, RET_DIM)),
          head_blk((1, RET_DIM)),
          pl.BlockSpec((1, RET_W), lambda bi, t: (0, 0)),
      ],
      out_specs=pl.BlockSpec((1, RET_ROWS, RET_W), lambda bi, t: (bi, t, 0)),
      scratch_shapes=[pltpu.VMEM((RET_HEADS, RET_DIM, RET_DIM), F32)],
      compiler_params=_compiler_params(("parallel", "arbitrary"), blocks,
                                       _nbytes((RET_HEADS, RET_DIM, RET_DIM), F32)),
      name="retention",
  )(proj3, proj3, proj3, proj3, cos, sin, decay, q_in, k_out, cdec, gn_gain)


def _mem_kv_kernel(m_ref, g_ref, w_ref, o_ref):
  n = _rmsnorm(m_ref[0], g_ref[...]).astype(BF16)
  o_ref[0] = _dot(n, w_ref[...]).astype(BF16)


def _mem_kv(mem, g, w):
  b = mem.shape[0]
  blocks = (_nbytes((MEM_LEN, D_MODEL), F32) + _nbytes((D_MODEL, 2 * MEM_W), BF16)
            + _nbytes((MEM_LEN, 2 * MEM_W), BF16))
  return pl.pallas_call(
      _mem_kv_kernel,
      out_shape=jax.ShapeDtypeStruct((b, MEM_LEN, 2 * MEM_W), BF16),
      grid=(b,),
      in_specs=[
          pl.BlockSpec((1, MEM_LEN, D_MODEL), lambda i: (i, 0, 0)),
          pl.BlockSpec((1, D_MODEL), lambda i: (0, 0)),
          pl.BlockSpec((D_MODEL, 2 * MEM_W), lambda i: (0, 0)),
      ],
      out_specs=pl.BlockSpec((1, MEM_LEN, 2 * MEM_W), lambda i: (i, 0, 0)),
      compiler_params=_compiler_params(("parallel",), blocks),
      name="mem_kv",
  )(mem, g, w)


def _mem_attn_kernel(q_ref, kv_ref, o_ref):
  dh = MEM_HEAD_DIM
  outs = []
  for h in range(MEM_HEADS):
    q = q_ref[0, :, h * dh:(h + 1) * dh]
    k = kv_ref[0, :, h * dh:(h + 1) * dh]
    v = kv_ref[0, :, MEM_W + h * dh:MEM_W + (h + 1) * dh]
    s = _dot_nt(q, k) * (dh ** -0.5)
    p = jnp.exp(s - jnp.max(s, axis=-1, keepdims=True))
    l = jnp.sum(p, axis=-1, keepdims=True)
    outs.append(_dot(p.astype(BF16), v) / l)
  o_ref[0] = jnp.concatenate(outs, axis=1).astype(o_ref.dtype)


def _mem_attn(proj3, kv):
  b, s, _ = proj3.shape
  blocks = (2 * _nbytes((ROW_TILE, MEM_W), BF16) + _nbytes((MEM_LEN, 2 * MEM_W), BF16))
  return pl.pallas_call(
      _mem_attn_kernel,
      out_shape=jax.ShapeDtypeStruct((b, s, MEM_W), BF16),
      grid=(b, s // ROW_TILE),
      in_specs=[
          pl.BlockSpec((1, ROW_TILE, MEM_W), lambda bi, t: (bi, t, _QM_COL // MEM_W)),
          pl.BlockSpec((1, MEM_LEN, 2 * MEM_W), lambda bi, t: (bi, 0, 0)),
      ],
      out_specs=pl.BlockSpec((1, ROW_TILE, MEM_W), lambda bi, t: (bi, t, 0)),
      compiler_params=_compiler_params(("parallel", "parallel"), blocks),
      name="mem_attn",
  )(proj3, kv)


def _merge_kernel(x_ref, ya_ref, yr_ref, ym_ref, za_ref, zr_ref, zm_ref, wa_ref, wr_ref, wm_ref,
                  wo_ref, o_ref):
  def gate(z_ref, proj):
    return jax.nn.sigmoid(z_ref[0].astype(F32)) * proj

  merged = (gate(za_ref, _dot_tn(ya_ref[0], wa_ref[...]))
            + gate(zr_ref, _dot(yr_ref[0], wr_ref[...]))
            + gate(zm_ref, _dot(ym_ref[0], wm_ref[...])))
  o_ref[0] = x_ref[0] + _dot(merged.astype(BF16), wo_ref[...])


def _merge(x, ya_t, yr, ym, proj3, wa, wr, wm, wo):
  b, s, _ = x.shape
  zblk = _Z_COL // D_MODEL
  row = lambda w, col=0: pl.BlockSpec((1, ROW_TILE, w), lambda bi, t: (bi, t, col))
  ya_spec = pl.BlockSpec((1, MOBA_W, ROW_TILE), lambda bi, t: (bi, 0, t))
  full = lambda shape: pl.BlockSpec(shape, lambda bi, t: (0, 0))
  blocks = (2 * _nbytes((ROW_TILE, D_MODEL), F32) + 3 * _nbytes((ROW_TILE, MOBA_W), BF16)
            + 3 * _nbytes((ROW_TILE, D_MODEL), BF16) + 3 * _nbytes((MOBA_W, D_MODEL), BF16)
            + _nbytes((D_MODEL, D_MODEL), BF16))
  return pl.pallas_call(
      _merge_kernel,
      out_shape=jax.ShapeDtypeStruct((b, s, D_MODEL), F32),
      grid=(b, s // ROW_TILE),
      in_specs=[
          row(D_MODEL), ya_spec, row(RET_W), row(MEM_W),
          row(D_MODEL, zblk), row(D_MODEL, zblk + 1), row(D_MODEL, zblk + 2),
          full((MOBA_W, D_MODEL)), full((RET_W, D_MODEL)), full((MEM_W, D_MODEL)),
          full((D_MODEL, D_MODEL)),
      ],
      out_specs=row(D_MODEL),
      compiler_params=_compiler_params(("parallel", "parallel"), blocks),
      name="merge",
  )(x, ya_t, yr, ym, proj3, proj3, proj3, wa, wr, wm, wo)


def _ffn_kernel(h_ref, gf_ref, wup_ref, cw_ref, cb_ref, wdn_ref, gfin_ref, o_ref, hid_ref,
                act_ref):
  tm = ROW_TILE
  halo = CONV_HALO
  cw = FFN_COL_CHUNK
  n_chunks = D_FF // cw

  @pl.when(pl.program_id(1) == 0)
  def _():
    hid_ref[:, 0:halo, :] = jnp.zeros((2 * n_chunks, halo, cw), F32)

  h1 = h_ref[0]
  n = _rmsnorm(h1, gf_ref[...]).astype(BF16)

  for c in range(n_chunks):
    for part in range(2):
      col = part * D_FF + c * cw
      hid_ref[2 * c + part, halo:halo + tm, :] = _dot(n, wup_ref[:, col:col + cw])

  top_row = lax.broadcasted_iota(jnp.int32, (halo, cw), 0) == 0

  def shift_down(v, top):
    rolled = pltpu.roll(v, 1, 0)
    return jnp.concatenate([jnp.where(top_row, top, rolled[0:halo]), rolled[halo:]], axis=0)

  assert CONV_WIDTH == 3
  acc = None
  done = 0
  for c in range(n_chunks):
    halves = []
    for part in range(2):
      col = part * D_FF + c * cw
      cols = slice(col, col + cw)
      buf = hid_ref.at[2 * c + part]
      x = buf[halo:halo + tm, :]
      last = buf[halo - 1:halo, :]
      before_last = buf[halo - 2:halo - 1, :]
      w0, w1, w2 = (cw_ref[tap:tap + 1, cols] for tap in range(CONV_WIDTH))
      a = shift_down(w0 * x, w0 * last)
      b = shift_down(w1 * x + a, w1 * last + w0 * before_last)
      halves.append((cb_ref[:, cols] + w2 * x) + b)
      buf[0:halo, :] = buf[tm:tm + halo, :]
    gate, up = halves
    act = gate * (1.0 + lax.erf(gate * math.sqrt(0.5))) * up
    act_ref[:, c * cw:(c + 1) * cw] = act.astype(BF16)
    if c + 1 in (FFN_DOWN_SPLIT, n_chunks):
      ks = slice(done * cw, (c + 1) * cw)
      part_sum = _dot(act_ref[:, ks], wdn_ref[ks, :])
      acc = part_sum if acc is None else acc + part_sum
      done = c + 1

  o_ref[0] = _rmsnorm(h1 + acc, gfin_ref[...])


def _ffn(h3, g_ffn, w_up, conv_w, conv_b, w_down, g_final):
  b, s, _ = h3.shape
  full = lambda shape: pl.BlockSpec(shape, lambda bi, t: (0, 0))
  once = lambda shape: pl.BlockSpec(shape, lambda bi, t: (0, 0), pipeline_mode=pl.Buffered(1))
  tiles = 2 * _nbytes((ROW_TILE, D_MODEL), F32) + _nbytes((CONV_WIDTH + 1, 2 * D_FF), F32)
  weights = _nbytes((D_MODEL, 2 * D_FF), BF16) + _nbytes((D_FF, D_MODEL), BF16)
  slots = 2 * D_FF // FFN_COL_CHUNK
  scratch = (_nbytes((slots, ROW_TILE + CONV_HALO, FFN_COL_CHUNK), F32)
             + _nbytes((ROW_TILE, D_FF), BF16))
  return pl.pallas_call(
      _ffn_kernel,
      out_shape=jax.ShapeDtypeStruct((b, s, D_MODEL), F32),
      grid=(b, s // ROW_TILE),
      in_specs=[
          pl.BlockSpec((1, ROW_TILE, D_MODEL), lambda bi, t: (bi, t, 0)),
          full((1, D_MODEL)),
          once((D_MODEL, 2 * D_FF)),
          full((CONV_WIDTH, 2 * D_FF)),
          full((1, 2 * D_FF)),
          once((D_FF, D_MODEL)),
          full((1, D_MODEL)),
      ],
      out_specs=pl.BlockSpec((1, ROW_TILE, D_MODEL), lambda bi, t: (bi, t, 0)),
      scratch_shapes=[
          pltpu.VMEM((slots, ROW_TILE + CONV_HALO, FFN_COL_CHUNK), F32),
          pltpu.VMEM((ROW_TILE, D_FF), BF16),
      ],
      compiler_params=_compiler_params(("parallel", "arbitrary"), tiles, weights + scratch),
      name="ffn",
  )(h3, g_ffn, w_up, conv_w, conv_b, w_down, g_final)


def kernel(x, mem, g_mix, w_in, rel_bias, ret_gn_gain, g_mem, w_mem_kv, w_br_attn, w_br_ret,
           w_br_mem, w_out, g_ffn, w_up, conv_w, conv_b, w_down, g_final):
  b, s, d = x.shape
  assert d == D_MODEL and mem.shape == (b, MEM_LEN, D_MODEL)
  assert s % MOBA_BLOCK == 0 and s % ROW_TILE == 0 and s % RET_ROWS == 0
  assert g_mix.shape[0] == 1, "single-layer block"
  h = x.reshape(b * s, d)
  bias = _bias_tiles(rel_bias, s // MOBA_BLOCK)
  proj = _in_proj(h, g_mix, w_in[0].astype(BF16))
  proj3 = proj.reshape(b, s, D_IN)
  ya_t = _moba(proj3, bias)
  y_r = _retention(proj3, ret_gn_gain)
  kv = _mem_kv(mem, g_mem, w_mem_kv[0].astype(BF16))
  y_m = _mem_attn(proj3, kv)
  h1 = _merge(x, ya_t, y_r, y_m, proj3, w_br_attn[0].astype(BF16), w_br_ret[0].astype(BF16),
              w_br_mem[0].astype(BF16), w_out[0].astype(BF16))
  return _ffn(h1, g_ffn, w_up[0].astype(BF16), conv_w[0], conv_b,
              (0.5 * w_down[0]).astype(BF16), g_final.reshape(1, d))
```

```python
import functools
import math

import jax
import jax.numpy as jnp
from jax import lax
from jax.experimental import pallas as pl
from jax.experimental.pallas import tpu as pltpu

F32 = jnp.float32
BF16 = jnp.bfloat16

D_MODEL = 1024
MEM_LEN = 256
MOBA_HEADS = 8
MOBA_HEAD_DIM = 64
MOBA_BLOCK = 256
MOBA_TOPK = 3
RET_HEADS = 4
RET_DIM = 128
RET_CHUNK = 128
ROPE_BASE = 10000.0
MEM_HEADS = 4
MEM_HEAD_DIM = 128
REL_BUCKETS = 32
REL_MAX_DIST = 2048
D_FF = 2816
CONV_WIDTH = 3
EPS = 1e-6
NEG_INF = -1e30
LOG2E = math.log2(math.e)
F32_LOWEST = float(jnp.finfo(jnp.float32).min)
F32_AS_BF16_PIECES = 3

MOBA_W = MOBA_HEADS * MOBA_HEAD_DIM
RET_W = RET_HEADS * RET_DIM
MEM_W = MEM_HEADS * MEM_HEAD_DIM
D_IN = 3 * MOBA_W + 4 * RET_W + MEM_W + 3 * D_MODEL

V7X_LANES = 128
V7X_SUBLANES = 8
V7X_VMEM_BYTES = 64 * 1024 * 1024
V7X_VMEM_RESERVE_BYTES = 6 * 1024 * 1024

_QA_BLK = 0
_KA_BLK = MOBA_W // V7X_LANES
_VA_BLK = 2 * MOBA_W // V7X_LANES
_QR_BLK = 3 * MOBA_W // V7X_LANES
_KR_BLK = _QR_BLK + RET_W // V7X_LANES
_VR_BLK = _KR_BLK + RET_W // V7X_LANES
_GR_BLK = _VR_BLK + RET_W // V7X_LANES
_QM_COL = 3 * MOBA_W + 4 * RET_W
_Z_COL = _QM_COL + MEM_W

ROW_TILE = 512
PROJ_COL_CHUNK = 512
RET_ROWS = 512
FFN_COL_CHUNK = 256
FFN_DOWN_SPLIT = 6
MOBA_STEP_BLOCKS = 4
MOBA_Q_BLOCKS = 2
CONV_HALO = V7X_SUBLANES


def _compiler_params(semantics, block_bytes, scratch_bytes=0, flags=None):
  need = 2 * block_bytes + scratch_bytes + 12 * 1024 * 1024
  limit = min(need, V7X_VMEM_BYTES - V7X_VMEM_RESERVE_BYTES)
  return pltpu.CompilerParams(dimension_semantics=semantics, vmem_limit_bytes=limit, flags=flags)


def _nbytes(shape, dtype):
  return math.prod(shape) * jnp.dtype(dtype).itemsize


def _rmsnorm(x, g):
  ms = jnp.mean(x * x, axis=-1, keepdims=True)
  return x * lax.rsqrt(ms + EPS) * g


def _dot(a, b):
  return jnp.dot(a, b, preferred_element_type=F32)


def _dot_nt(a, b):
  return lax.dot_general(a, b, (((1,), (1,)), ((), ())), preferred_element_type=F32)


def _dot_tn(a, b):
  return lax.dot_general(a, b, (((0,), (0,)), ((), ())), preferred_element_type=F32)


def _in_proj_kernel(x_ref, g_ref, w_ref, o_ref):
  n = _rmsnorm(x_ref[...], g_ref[...]).astype(BF16)
  for c in range(D_IN // PROJ_COL_CHUNK):
    cols = slice(c * PROJ_COL_CHUNK, (c + 1) * PROJ_COL_CHUNK)
    o_ref[:, cols] = _dot(n, w_ref[:, cols]).astype(BF16)


def _in_proj(x2d, g, w):
  t = x2d.shape[0]
  blocks = (_nbytes((ROW_TILE, D_MODEL), F32) + _nbytes((D_MODEL, D_IN), BF16)
            + _nbytes((ROW_TILE, D_IN), BF16))
  return pl.pallas_call(
      _in_proj_kernel,
      out_shape=jax.ShapeDtypeStruct((t, D_IN), BF16),
      grid=(t // ROW_TILE,),
      in_specs=[
          pl.BlockSpec((ROW_TILE, D_MODEL), lambda i: (i, 0)),
          pl.BlockSpec((1, D_MODEL), lambda i: (0, 0)),
          pl.BlockSpec((D_MODEL, D_IN), lambda i: (0, 0)),
      ],
      out_specs=pl.BlockSpec((ROW_TILE, D_IN), lambda i: (i, 0)),
      compiler_params=_compiler_params(("parallel",), blocks),
      name="in_proj",
  )(x2d, g, w)


def _rel_bucket(dist):
  max_exact = REL_BUCKETS // 2
  d = jnp.maximum(dist, 0)
  df = jnp.maximum(d, 1).astype(F32)
  large = max_exact + (jnp.log(df / max_exact) / math.log(REL_MAX_DIST / max_exact)
                       * (REL_BUCKETS - max_exact)).astype(jnp.int32)
  large = jnp.minimum(large, REL_BUCKETS - 1)
  return jnp.where(d < max_exact, d, large)


def _write_bias_tiles(tab_ref, bkt_ref, h, o_ref):
  nb = o_ref.shape[0]
  bkt = bkt_ref[...]
  w = jnp.zeros(bkt.shape, F32)
  for b in range(REL_BUCKETS):
    w = jnp.where(bkt == b, tab_ref[b, h], w)
  c = lax.broadcasted_iota(jnp.int32, (MOBA_BLOCK, MOBA_BLOCK), 0)
  r = lax.broadcasted_iota(jnp.int32, (MOBA_BLOCK, MOBA_BLOCK), 1)
  for delta in range(nb):
    wide = jnp.broadcast_to(w[delta:delta + 1], (MOBA_BLOCK, 2 * MOBA_BLOCK))
    tile = pltpu.roll(wide, MOBA_BLOCK + 1, 1, stride=1, stride_axis=0)[:, :MOBA_BLOCK]
    tile = tile * LOG2E
    o_ref[delta] = jnp.where(r >= c, tile, NEG_INF) if delta == 0 else tile


def _bias_buckets(nb):
  m = jnp.arange(2 * MOBA_BLOCK, dtype=jnp.int32)
  dist = jnp.arange(nb, dtype=jnp.int32)[:, None] * MOBA_BLOCK + m[None, :] - (MOBA_BLOCK - 1)
  return _rel_bucket(dist)


def _moba_kernel(tab_ref, bkt_ref, q_ref, qall_ref, k_ref, v_ref, o_ref, bias_ref, pen_ref, s_ref,
                 *, nb):
  i = pl.program_id(2)

  @pl.when((pl.program_id(1) == 0) & (i == 0))
  def _():
    for hh in range(V7X_LANES // MOBA_HEAD_DIM):
      head = pl.program_id(0) * (V7X_LANES // MOBA_HEAD_DIM) + hh
      _write_bias_tiles(tab_ref, bkt_ref, head, bias_ref.at[hh])

  blk = MOBA_BLOCK
  dh = MOBA_HEAD_DIM
  heads = V7X_LANES // dh
  assert heads == 2, "each head borrows the other head's lanes for its probability sums"
  seq = nb * blk
  sub = V7X_SUBLANES

  def head_lanes(hh, shape):
    lane = lax.broadcasted_iota(jnp.int32, shape, len(shape) - 1)
    return (lane >= hh * dh) & (lane < (hh + 1) * dh)

  @pl.when(i == 0)
  def _():
    kmean = jnp.concatenate(
        [jnp.mean(k_ref[0, j * blk:(j + 1) * blk, :].astype(F32), axis=0, keepdims=True)
         for j in range(nb)], axis=0)
    pieces, rest = [], kmean
    for _ in range(F32_AS_BF16_PIECES):
      piece = rest.astype(BF16)
      pieces.append(piece)
      rest = rest - piece.astype(F32)
    zero = jnp.zeros_like(pieces[0])
    stacked = jnp.concatenate([jnp.where(head_lanes(hh, piece.shape), piece, zero)
                               for hh in range(heads) for piece in pieces], axis=0)
    gates = _dot_nt(stacked, qall_ref[0])
    jidx = lax.broadcasted_iota(jnp.int32, (nb, seq), 0)
    qblk = lax.broadcasted_iota(jnp.int32, (nb, seq), 1) // blk
    past = jidx < qblk
    for hh in range(heads):
      base = hh * F32_AS_BF16_PIECES * nb
      gate = sum(gates[base + p * nb:base + (p + 1) * nb] for p in range(F32_AS_BF16_PIECES))
      gate = jnp.where(past, gate, NEG_INF)
      taken = jnp.zeros(gate.shape, F32)
      for _ in range(MOBA_TOPK):
        best = jnp.max(gate, axis=0, keepdims=True)
        first = jnp.min(jnp.where(gate == best, jidx, nb), axis=0, keepdims=True)
        hit = jidx == first
        taken = jnp.where(hit, 1.0, taken)
        gate = jnp.where(hit, F32_LOWEST, gate)
      pen_ref[hh] = jnp.where(past, jnp.where(taken > 0.0, 0.0, NEG_INF),
                              jnp.where(jidx == qblk, 0.0, NEG_INF))

  qt = MOBA_Q_BLOCKS * blk
  q = q_ref[0].astype(F32) * (dh ** -0.5)
  qz = [jnp.where(head_lanes(hh, q.shape), q, 0.0).astype(BF16) for hh in range(heads)]
  qcols = pl.ds(pl.multiple_of(i * qt, qt), qt)
  first_qblk = i * MOBA_Q_BLOCKS

  step = MOBA_STEP_BLOCKS * blk
  trips = (first_qblk + MOBA_Q_BLOCKS - 1 + MOBA_STEP_BLOCKS) // MOBA_STEP_BLOCKS

  def fold(x):
    return x.reshape(blk // sub, sub, qt)

  def score_pass(t, run_max):
    start = pl.multiple_of(t * step, step)
    k = k_ref[0, pl.ds(start, step), :]
    out = []
    for hh in range(heads):
      s = _dot_nt(k, qz[hh])
      mx = run_max[hh]
      for u in range(MOBA_STEP_BLOCKS):
        j = t * MOBA_STEP_BLOCKS + u
        bias = jnp.concatenate(
            [bias_ref[hh, jnp.maximum(first_qblk + qb - j, 0)] for qb in range(MOBA_Q_BLOCKS)],
            axis=1)
        part = s[u * blk:(u + 1) * blk] * LOG2E + bias
        s_ref[hh, pl.ds(start + u * blk, blk), :] = part
        mx = jnp.maximum(mx, jnp.max(fold(part), axis=0) + pen_ref[hh, pl.ds(j, 1), qcols])
      out.append(mx)
    return tuple(out)

  run_max = lax.fori_loop(0, trips, score_pass,
                          tuple(jnp.full((sub, qt), NEG_INF, F32) for _ in range(heads)))
  row_max = [jnp.max(mx, axis=0, keepdims=True) for mx in run_max]

  def value_pass(t, carry):
    start = pl.multiple_of(t * step, step)
    v = v_ref[0, pl.ds(start, step), :]
    out = []
    for hh in range(heads):
      lsum, acc = carry[hh]
      probs = []
      for u in range(MOBA_STEP_BLOCKS):
        j = t * MOBA_STEP_BLOCKS + u
        shift = row_max[hh] - pen_ref[hh, pl.ds(j, 1), qcols]
        p = jnp.exp2(s_ref[hh, pl.ds(start + u * blk, blk), :] - shift)
        probs.append(p.astype(BF16))
      v_ones = jnp.where(head_lanes(hh, v.shape), v, jnp.ones_like(v))
      pv = _dot_tn(v_ones, jnp.concatenate(probs, axis=0))
      other = (heads - 1 - hh) * dh
      out.append((lsum + pv[other:other + sub], acc + pv[hh * dh:(hh + 1) * dh]))
    return tuple(out)

  final = lax.fori_loop(
      0, trips, value_pass,
      tuple((jnp.zeros((sub, qt), F32), jnp.zeros((dh, qt), F32)) for _ in range(heads)))
  for hh, (lsum, acc) in enumerate(final):
    o_ref[0, hh * dh:(hh + 1) * dh, :] = (acc / lsum[0:1]).astype(o_ref.dtype)


def _moba(proj3, rel_bias):
  b, s, _ = proj3.shape
  nb = s // MOBA_BLOCK
  assert nb % MOBA_STEP_BLOCKS == 0, "the inner loops may touch blocks past the own block"
  assert nb % MOBA_Q_BLOCKS == 0
  pairs = MOBA_W // V7X_LANES
  per_pair = V7X_LANES // MOBA_HEAD_DIM
  qt = MOBA_Q_BLOCKS * MOBA_BLOCK
  blocks = (2 * _nbytes((qt, V7X_LANES), BF16) + 3 * _nbytes((s, V7X_LANES), BF16)
            + _nbytes((nb, 2 * MOBA_BLOCK), jnp.int32))
  scratch = (_nbytes((per_pair, nb, MOBA_BLOCK, MOBA_BLOCK), F32) + _nbytes((per_pair, nb, s), F32)
             + _nbytes((per_pair, s, qt), F32))
  seq_blk = lambda off: pl.BlockSpec((1, s, V7X_LANES), lambda hp, bi, i: (bi, 0, off + hp))
  return pl.pallas_call(
      functools.partial(_moba_kernel, nb=nb),
      out_shape=jax.ShapeDtypeStruct((b, MOBA_W, s), BF16),
      grid=(pairs, b, nb // MOBA_Q_BLOCKS),
      in_specs=[
          pl.BlockSpec(memory_space=pltpu.SMEM),
          pl.BlockSpec((nb, 2 * MOBA_BLOCK), lambda hp, bi, i: (0, 0)),
          pl.BlockSpec((1, qt, V7X_LANES), lambda hp, bi, i: (bi, i, _QA_BLK + hp)),
          seq_blk(_QA_BLK), seq_blk(_KA_BLK), seq_blk(_VA_BLK),
      ],
      out_specs=pl.BlockSpec((1, V7X_LANES, qt), lambda hp, bi, i: (bi, hp, i)),
      scratch_shapes=[
          pltpu.VMEM((per_pair, nb, MOBA_BLOCK, MOBA_BLOCK), F32),
          pltpu.VMEM((per_pair, nb, s), F32),
          pltpu.VMEM((per_pair, s, qt), F32),
      ],
      compiler_params=_compiler_params(("arbitrary", "arbitrary", "arbitrary"), blocks, scratch),
      name="moba",
  )(rel_bias, _bias_buckets(nb), proj3, proj3, proj3, proj3)


def _retention_kernel(q_ref, k_ref, v_ref, g_ref, cos_ref, sin_ref, decay_ref, qin_ref, kout_ref,
                      cdec_ref, gain_ref, o_ref, state_ref):
  c_len = RET_CHUNK
  half = RET_DIM // 2

  @pl.when(pl.program_id(1) == 0)
  def _():
    state_ref[...] = jnp.zeros(state_ref.shape, F32)

  for c in range(RET_ROWS // c_len):
    rows = slice(c * c_len, (c + 1) * c_len)
    cos = cos_ref[rows, :]
    sin = sin_ref[rows, :]
    for h in range(RET_HEADS):
      lanes = slice(h * RET_DIM, (h + 1) * RET_DIM)
      q = q_ref[0, rows, lanes].astype(F32)
      k = k_ref[0, rows, lanes].astype(F32)
      v = v_ref[0, rows, lanes]
      qr = q * cos + pltpu.roll(q, half, 1) * sin
      kr = (k * cos + pltpu.roll(k, half, 1) * sin) * (RET_DIM ** -0.5)
      state = state_ref[h]
      sc = _dot_nt(qr.astype(BF16), kr.astype(BF16)) * decay_ref[h]
      inner = _dot(sc.astype(BF16), v)
      cross = _dot((qr * qin_ref[h]).astype(BF16), state.astype(BF16))
      kv = _dot_tn((kr * kout_ref[h]).astype(BF16), v)
      state_ref[h] = cdec_ref[h] * state + kv
      y = inner + cross
      mu = jnp.mean(y, axis=-1, keepdims=True)
      yc = y - mu
      var = jnp.mean(yc * yc, axis=-1, keepdims=True)
      yn = yc * lax.rsqrt(var + EPS) * gain_ref[:, lanes]
      g = g_ref[0, rows, lanes].astype(F32)
      o_ref[0, rows, lanes] = (yn * (g * jax.nn.sigmoid(g))).astype(o_ref.dtype)


def _retention(proj3, gn_gain):
  b, s, _ = proj3.shape
  c_len = RET_CHUNK
  half = RET_DIM // 2
  pos = jnp.arange(s, dtype=F32)
  inv = ROPE_BASE ** (-jnp.arange(half, dtype=F32) / half)
  ang = pos[:, None] * inv
  cos = jnp.concatenate([jnp.cos(ang), jnp.cos(ang)], axis=-1)
  sin = jnp.concatenate([-jnp.sin(ang), jnp.sin(ang)], axis=-1)
  log_gamma = jnp.log1p(-jnp.power(2.0, -5.0 - jnp.arange(RET_HEADS, dtype=F32)))
  idx = jnp.arange(c_len, dtype=F32)
  diff = idx[:, None] - idx[None, :]
  decay = jnp.where(diff >= 0, jnp.exp(log_gamma[:, None, None] * jnp.maximum(diff, 0.0)), 0.0)
  lanes = (RET_HEADS, c_len, RET_DIM)
  q_in = jnp.broadcast_to(jnp.exp(log_gamma[:, None] * (idx + 1.0))[:, :, None], lanes)
  k_out = jnp.broadcast_to(jnp.exp(log_gamma[:, None] * (c_len - 1.0 - idx))[:, :, None], lanes)
  cdec = jnp.broadcast_to(jnp.exp(log_gamma * c_len)[:, None, None], (RET_HEADS, 1, RET_DIM))

  first = _QR_BLK * V7X_LANES // RET_W
  row_blk = lambda off: pl.BlockSpec((1, RET_ROWS, RET_W), lambda bi, t: (bi, t, first + off))
  head_blk = lambda shape: pl.BlockSpec((RET_HEADS,) + shape, lambda bi, t: (0, 0, 0))
  blocks = (5 * _nbytes((RET_ROWS, RET_W), BF16) + 2 * _nbytes((RET_ROWS, RET_DIM), F32)
            + 3 * _nbytes((RET_HEADS, c_len, RET_DIM), F32))
  return pl.pallas_call(
      _retention_kernel,
      out_shape=jax.ShapeDtypeStruct((b, s, RET_W), BF16),
      grid=(b, s // RET_ROWS),
      in_specs=[
          row_blk(0), row_blk(1), row_blk(2), row_blk(3),
          pl.BlockSpec((RET_ROWS, RET_DIM), lambda bi, t: (t, 0)),
          pl.BlockSpec((RET_ROWS, RET_DIM), lambda bi, t: (t, 0)),
          head_blk((c_len, c_len)), head_blk((c_len, RET_DIM)), head_blk((c_len, RET_DIM)),
          head_blk((1, RET_DIM)),
          pl.BlockSpec((1, RET_W), lambda bi, t: (0, 0)),
      ],
      out_specs=pl.BlockSpec((1, RET_ROWS, RET_W), lambda bi, t: (bi, t, 0)),
      scratch_shapes=[pltpu.VMEM((RET_HEADS, RET_DIM, RET_DIM), F32)],
      compiler_params=_compiler_params(("parallel", "arbitrary"), blocks,
                                       _nbytes((RET_HEADS, RET_DIM, RET_DIM), F32)),
      name="retention",
  )(proj3, proj3, proj3, proj3, cos, sin, decay, q_in, k_out, cdec, gn_gain)


def _mem_kv_kernel(m_ref, g_ref, w_ref, o_ref):
  n = _rmsnorm(m_ref[0], g_ref[...]).astype(BF16)
  o_ref[0] = _dot(n, w_ref[...]).astype(BF16)


def _mem_kv(mem, g, w):
  b = mem.shape[0]
  blocks = (_nbytes((MEM_LEN, D_MODEL), F32) + _nbytes((D_MODEL, 2 * MEM_W), BF16)
            + _nbytes((MEM_LEN, 2 * MEM_W), BF16))
  return pl.pallas_call(
      _mem_kv_kernel,
      out_shape=jax.ShapeDtypeStruct((b, MEM_LEN, 2 * MEM_W), BF16),
      grid=(b,),
      in_specs=[
          pl.BlockSpec((1, MEM_LEN, D_MODEL), lambda i: (i, 0, 0)),
          pl.BlockSpec((1, D_MODEL), lambda i: (0, 0)),
          pl.BlockSpec((D_MODEL, 2 * MEM_W), lambda i: (0, 0)),
      ],
      out_specs=pl.BlockSpec((1, MEM_LEN, 2 * MEM_W), lambda i: (i, 0, 0)),
      compiler_params=_compiler_params(("parallel",), blocks),
      name="mem_kv",
  )(mem, g, w)


def _mem_attention(q_ref, kv_ref):
  dh = MEM_HEAD_DIM
  outs = []
  for h in range(MEM_HEADS):
    q = q_ref[0, :, h * dh:(h + 1) * dh]
    k = kv_ref[0, :, h * dh:(h + 1) * dh]
    v = kv_ref[0, :, MEM_W + h * dh:MEM_W + (h + 1) * dh]
    s = _dot_nt(q, k) * (dh ** -0.5)
    p = jnp.exp(s - jnp.max(s, axis=-1, keepdims=True))
    l = jnp.sum(p, axis=-1, keepdims=True)
    outs.append(_dot(p.astype(BF16), v) / l)
  return jnp.concatenate(outs, axis=1).astype(BF16)


def _merge_kernel(x_ref, ya_ref, yr_ref, qm_ref, kv_ref, za_ref, zr_ref, zm_ref, wa_ref, wr_ref,
                  wm_ref, wo_ref, o_ref):
  def gate(z_ref, proj):
    return jax.nn.sigmoid(z_ref[0].astype(F32)) * proj

  merged = (gate(za_ref, _dot_tn(ya_ref[0], wa_ref[...]))
            + gate(zr_ref, _dot(yr_ref[0], wr_ref[...]))
            + gate(zm_ref, _dot(_mem_attention(qm_ref, kv_ref), wm_ref[...])))
  o_ref[0] = x_ref[0] + _dot(merged.astype(BF16), wo_ref[...])


def _merge(x, ya_t, yr, kv, proj3, wa, wr, wm, wo):
  b, s, _ = x.shape
  zblk = _Z_COL // D_MODEL
  row = lambda w, col=0: pl.BlockSpec((1, ROW_TILE, w), lambda bi, t: (bi, t, col))
  ya_spec = pl.BlockSpec((1, MOBA_W, ROW_TILE), lambda bi, t: (bi, 0, t))
  full = lambda shape: pl.BlockSpec(shape, lambda bi, t: (0, 0))
  blocks = (2 * _nbytes((ROW_TILE, D_MODEL), F32) + 3 * _nbytes((ROW_TILE, MOBA_W), BF16)
            + 3 * _nbytes((ROW_TILE, D_MODEL), BF16) + 3 * _nbytes((MOBA_W, D_MODEL), BF16)
            + _nbytes((D_MODEL, D_MODEL), BF16) + _nbytes((MEM_LEN, 2 * MEM_W), BF16))
  return pl.pallas_call(
      _merge_kernel,
      out_shape=jax.ShapeDtypeStruct((b, s, D_MODEL), F32),
      grid=(b, s // ROW_TILE),
      in_specs=[
          row(D_MODEL), ya_spec, row(RET_W), row(MEM_W, _QM_COL // MEM_W),
          pl.BlockSpec((1, MEM_LEN, 2 * MEM_W), lambda bi, t: (bi, 0, 0)),
          row(D_MODEL, zblk), row(D_MODEL, zblk + 1), row(D_MODEL, zblk + 2),
          full((MOBA_W, D_MODEL)), full((RET_W, D_MODEL)), full((MEM_W, D_MODEL)),
          full((D_MODEL, D_MODEL)),
      ],
      out_specs=row(D_MODEL),
      compiler_params=_compiler_params(("parallel", "parallel"), blocks),
      name="merge",
  )(x, ya_t, yr, proj3, kv, proj3, proj3, proj3, wa, wr, wm, wo)


def _ffn_kernel(h_ref, gf_ref, wup_ref, cw_ref, cb_ref, wdn_ref, gfin_ref, o_ref, hid_ref,
                act_ref):
  tm = ROW_TILE
  halo = CONV_HALO
  cw = FFN_COL_CHUNK
  n_chunks = D_FF // cw

  @pl.when(pl.program_id(1) == 0)
  def _():
    hid_ref[:, 0:halo, :] = jnp.zeros((2 * n_chunks, halo, cw), F32)

  h1 = h_ref[0]
  n = _rmsnorm(h1, gf_ref[...]).astype(BF16)

  for c in range(n_chunks):
    for part in range(2):
      col = part * D_FF + c * cw
      hid_ref[2 * c + part, halo:halo + tm, :] = _dot(n, wup_ref[:, col:col + cw])

  top_row = lax.broadcasted_iota(jnp.int32, (halo, cw), 0) == 0

  def shift_down(v, top):
    rolled = pltpu.roll(v, 1, 0)
    return jnp.concatenate([jnp.where(top_row, top, rolled[0:halo]), rolled[halo:]], axis=0)

  assert CONV_WIDTH == 3
  acc = None
  done = 0
  for c in range(n_chunks):
    halves = []
    for part in range(2):
      col = part * D_FF + c * cw
      cols = slice(col, col + cw)
      buf = hid_ref.at[2 * c + part]
      x = buf[halo:halo + tm, :]
      last = buf[halo - 1:halo, :]
      before_last = buf[halo - 2:halo - 1, :]
      w0, w1, w2 = (cw_ref[tap:tap + 1, cols] for tap in range(CONV_WIDTH))
      a = shift_down(w0 * x, w0 * last)
      b = shift_down(w1 * x + a, w1 * last + w0 * before_last)
      halves.append((cb_ref[:, cols] + w2 * x) + b)
      buf[0:halo, :] = buf[tm:tm + halo, :]
    gate, up = halves
    act = gate * (1.0 + lax.erf(gate * math.sqrt(0.5))) * up
    act_ref[:, c * cw:(c + 1) * cw] = act.astype(BF16)
    if c + 1 in (FFN_DOWN_SPLIT, n_chunks):
      ks = slice(done * cw, (c + 1) * cw)
      part_sum = _dot(act_ref[:, ks], wdn_ref[ks, :])
      acc = part_sum if acc is None else acc + part_sum
      done = c + 1

  o_ref[0] = _rmsnorm(h1 + acc, gfin_ref[...])


def _ffn(h3, g_ffn, w_up, conv_w, conv_b, w_down, g_final):
  b, s, _ = h3.shape
  full = lambda shape: pl.BlockSpec(shape, lambda bi, t: (0, 0))
  once = lambda shape: pl.BlockSpec(shape, lambda bi, t: (0, 0), pipeline_mode=pl.Buffered(1))
  tiles = 2 * _nbytes((ROW_TILE, D_MODEL), F32) + _nbytes((CONV_WIDTH + 1, 2 * D_FF), F32)
  weights = _nbytes((D_MODEL, 2 * D_FF), BF16) + _nbytes((D_FF, D_MODEL), BF16)
  slots = 2 * D_FF // FFN_COL_CHUNK
  scratch = (_nbytes((slots, ROW_TILE + CONV_HALO, FFN_COL_CHUNK), F32)
             + _nbytes((ROW_TILE, D_FF), BF16))
  return pl.pallas_call(
      _ffn_kernel,
      out_shape=jax.ShapeDtypeStruct((b, s, D_MODEL), F32),
      grid=(b, s // ROW_TILE),
      in_specs=[
          pl.BlockSpec((1, ROW_TILE, D_MODEL), lambda bi, t: (bi, t, 0)),
          full((1, D_MODEL)),
          once((D_MODEL, 2 * D_FF)),
          full((CONV_WIDTH, 2 * D_FF)),
          full((1, 2 * D_FF)),
          once((D_FF, D_MODEL)),
          full((1, D_MODEL)),
      ],
      out_specs=pl.BlockSpec((1, ROW_TILE, D_MODEL), lambda bi, t: (bi, t, 0)),
      scratch_shapes=[
          pltpu.VMEM((slots, ROW_TILE + CONV_HALO, FFN_COL_CHUNK), F32),
          pltpu.VMEM((ROW_TILE, D_FF), BF16),
      ],
      compiler_params=_compiler_params(("parallel", "arbitrary"), tiles, weights + scratch),
      name="ffn",
  )(h3, g_ffn, w_up, conv_w, conv_b, w_down, g_final)


def kernel(x, mem, g_mix, w_in, rel_bias, ret_gn_gain, g_mem, w_mem_kv, w_br_attn, w_br_ret,
           w_br_mem, w_out, g_ffn, w_up, conv_w, conv_b, w_down, g_final):
  b, s, d = x.shape
  assert d == D_MODEL and mem.shape == (b, MEM_LEN, D_MODEL)
  assert s % MOBA_BLOCK == 0 and s % ROW_TILE == 0 and s % RET_ROWS == 0
  assert g_mix.shape[0] == 1, "single-layer block"
  h = x.reshape(b * s, d)
  proj = _in_proj(h, g_mix, w_in[0].astype(BF16))
  proj3 = proj.reshape(b, s, D_IN)
  ya_t = _moba(proj3, rel_bias)
  y_r = _retention(proj3, ret_gn_gain)
  kv = _mem_kv(mem, g_mem, w_mem_kv[0].astype(BF16))
  h1 = _merge(x, ya_t, y_r, kv, proj3, w_br_attn[0].astype(BF16), w_br_ret[0].astype(BF16),
              w_br_mem[0].astype(BF16), w_out[0].astype(BF16))
  return _ffn(h1, g_ffn, w_up[0].astype(BF16), conv_w[0], conv_b,
              (0.5 * w_down[0]).astype(BF16), g_final.reshape(1, d))
```

```python
import functools
import math

import jax
import jax.numpy as jnp
from jax import lax
from jax.experimental import pallas as pl
from jax.experimental.pallas import tpu as pltpu

F32 = jnp.float32
BF16 = jnp.bfloat16

D_MODEL = 1024
MEM_LEN = 256
MOBA_HEADS = 8
MOBA_HEAD_DIM = 64
MOBA_BLOCK = 256
MOBA_TOPK = 3
RET_HEADS = 4
RET_DIM = 128
RET_CHUNK = 128
ROPE_BASE = 10000.0
MEM_HEADS = 4
MEM_HEAD_DIM = 128
REL_BUCKETS = 32
REL_MAX_DIST = 2048
D_FF = 2816
CONV_WIDTH = 3
EPS = 1e-6
NEG_INF = -1e30
LOG2E = math.log2(math.e)
F32_LOWEST = float(jnp.finfo(jnp.float32).min)
F32_AS_BF16_PIECES = 3

MOBA_W = MOBA_HEADS * MOBA_HEAD_DIM
RET_W = RET_HEADS * RET_DIM
MEM_W = MEM_HEADS * MEM_HEAD_DIM
D_IN = 3 * MOBA_W + 4 * RET_W + MEM_W + 3 * D_MODEL

V7X_LANES = 128
V7X_SUBLANES = 8
V7X_VMEM_BYTES = 64 * 1024 * 1024
V7X_VMEM_RESERVE_BYTES = 6 * 1024 * 1024

_QA_BLK = 0
_KA_BLK = MOBA_W // V7X_LANES
_VA_BLK = 2 * MOBA_W // V7X_LANES
_QR_BLK = 3 * MOBA_W // V7X_LANES
_KR_BLK = _QR_BLK + RET_W // V7X_LANES
_VR_BLK = _KR_BLK + RET_W // V7X_LANES
_GR_BLK = _VR_BLK + RET_W // V7X_LANES
_QM_COL = 3 * MOBA_W + 4 * RET_W
_Z_COL = _QM_COL + MEM_W

ROW_TILE = 512
PROJ_COL_CHUNK = 512
RET_ROWS = 512
FFN_COL_CHUNK = 256
FFN_DOWN_SPLIT = 6
MOBA_STEP_BLOCKS = 4
MOBA_TAIL_BLOCKS = 2
MOBA_Q_BLOCKS = 2
CONV_HALO = V7X_SUBLANES


def _compiler_params(semantics, block_bytes, scratch_bytes=0, flags=None):
  need = 2 * block_bytes + scratch_bytes + 12 * 1024 * 1024
  limit = min(need, V7X_VMEM_BYTES - V7X_VMEM_RESERVE_BYTES)
  return pltpu.CompilerParams(dimension_semantics=semantics, vmem_limit_bytes=limit, flags=flags)


def _nbytes(shape, dtype):
  return math.prod(shape) * jnp.dtype(dtype).itemsize


def _rmsnorm(x, g):
  ms = jnp.mean(x * x, axis=-1, keepdims=True)
  return x * lax.rsqrt(ms + EPS) * g


def _dot(a, b):
  return jnp.dot(a, b, preferred_element_type=F32)


def _dot_nt(a, b):
  return lax.dot_general(a, b, (((1,), (1,)), ((), ())), preferred_element_type=F32)


def _dot_tn(a, b):
  return lax.dot_general(a, b, (((0,), (0,)), ((), ())), preferred_element_type=F32)


def _in_proj_kernel(x_ref, g_ref, w_ref, o_ref):
  n = _rmsnorm(x_ref[...], g_ref[...]).astype(BF16)
  for c in range(D_IN // PROJ_COL_CHUNK):
    cols = slice(c * PROJ_COL_CHUNK, (c + 1) * PROJ_COL_CHUNK)
    o_ref[:, cols] = _dot(n, w_ref[:, cols]).astype(BF16)


def _in_proj(x2d, g, w):
  t = x2d.shape[0]
  blocks = (_nbytes((ROW_TILE, D_MODEL), F32) + _nbytes((D_MODEL, D_IN), BF16)
            + _nbytes((ROW_TILE, D_IN), BF16))
  return pl.pallas_call(
      _in_proj_kernel,
      out_shape=jax.ShapeDtypeStruct((t, D_IN), BF16),
      grid=(t // ROW_TILE,),
      in_specs=[
          pl.BlockSpec((ROW_TILE, D_MODEL), lambda i: (i, 0)),
          pl.BlockSpec((1, D_MODEL), lambda i: (0, 0)),
          pl.BlockSpec((D_MODEL, D_IN), lambda i: (0, 0)),
      ],
      out_specs=pl.BlockSpec((ROW_TILE, D_IN), lambda i: (i, 0)),
      compiler_params=_compiler_params(("parallel",), blocks),
      name="in_proj",
  )(x2d, g, w)


def _rel_bucket(dist):
  max_exact = REL_BUCKETS // 2
  d = jnp.maximum(dist, 0)
  df = jnp.maximum(d, 1).astype(F32)
  large = max_exact + (jnp.log(df / max_exact) / math.log(REL_MAX_DIST / max_exact)
                       * (REL_BUCKETS - max_exact)).astype(jnp.int32)
  large = jnp.minimum(large, REL_BUCKETS - 1)
  return jnp.where(d < max_exact, d, large)


def _write_bias_tiles(tab_ref, bkt_ref, h, o_ref):
  nb = o_ref.shape[0]
  bkt = bkt_ref[...]
  w = jnp.zeros(bkt.shape, F32)
  for b in range(REL_BUCKETS):
    w = jnp.where(bkt == b, tab_ref[b, h], w)
  c = lax.broadcasted_iota(jnp.int32, (MOBA_BLOCK, MOBA_BLOCK), 0)
  r = lax.broadcasted_iota(jnp.int32, (MOBA_BLOCK, MOBA_BLOCK), 1)
  for delta in range(nb):
    wide = jnp.broadcast_to(w[delta:delta + 1], (MOBA_BLOCK, 2 * MOBA_BLOCK))
    tile = pltpu.roll(wide, MOBA_BLOCK + 1, 1, stride=1, stride_axis=0)[:, :MOBA_BLOCK]
    tile = tile * LOG2E
    o_ref[delta] = jnp.where(r >= c, tile, NEG_INF) if delta == 0 else tile


def _bias_buckets(nb):
  m = jnp.arange(2 * MOBA_BLOCK, dtype=jnp.int32)
  dist = jnp.arange(nb, dtype=jnp.int32)[:, None] * MOBA_BLOCK + m[None, :] - (MOBA_BLOCK - 1)
  return _rel_bucket(dist)


def _moba_kernel(tab_ref, bkt_ref, q_ref, qall_ref, k_ref, v_ref, o_ref, bias_ref, pen_ref, s_ref,
                 *, nb):
  i = pl.program_id(2)

  @pl.when((pl.program_id(1) == 0) & (i == 0))
  def _():
    for hh in range(V7X_LANES // MOBA_HEAD_DIM):
      head = pl.program_id(0) * (V7X_LANES // MOBA_HEAD_DIM) + hh
      _write_bias_tiles(tab_ref, bkt_ref, head, bias_ref.at[hh])

  blk = MOBA_BLOCK
  dh = MOBA_HEAD_DIM
  heads = V7X_LANES // dh
  assert heads == 2, "each head borrows the other head's lanes for its probability sums"
  seq = nb * blk
  sub = V7X_SUBLANES

  def head_lanes(hh, shape):
    lane = lax.broadcasted_iota(jnp.int32, shape, len(shape) - 1)
    return (lane >= hh * dh) & (lane < (hh + 1) * dh)

  @pl.when(i == 0)
  def _():
    kmean = jnp.concatenate(
        [jnp.mean(k_ref[0, j * blk:(j + 1) * blk, :].astype(F32), axis=0, keepdims=True)
         for j in range(nb)], axis=0)
    pieces, rest = [], kmean
    for _ in range(F32_AS_BF16_PIECES):
      piece = rest.astype(BF16)
      pieces.append(piece)
      rest = rest - piece.astype(F32)
    zero = jnp.zeros_like(pieces[0])
    stacked = jnp.concatenate([jnp.where(head_lanes(hh, piece.shape), piece, zero)
                               for hh in range(heads) for piece in pieces], axis=0)
    gates = _dot_nt(stacked, qall_ref[0])
    jidx = lax.broadcasted_iota(jnp.int32, (nb, seq), 0)
    qblk = lax.broadcasted_iota(jnp.int32, (nb, seq), 1) // blk
    past = jidx < qblk
    for hh in range(heads):
      base = hh * F32_AS_BF16_PIECES * nb
      gate = sum(gates[base + p * nb:base + (p + 1) * nb] for p in range(F32_AS_BF16_PIECES))
      gate = jnp.where(past, gate, NEG_INF)
      taken = jnp.zeros(gate.shape, F32)
      for _ in range(MOBA_TOPK):
        best = jnp.max(gate, axis=0, keepdims=True)
        first = jnp.min(jnp.where(gate == best, jidx, nb), axis=0, keepdims=True)
        hit = jidx == first
        taken = jnp.where(hit, 1.0, taken)
        gate = jnp.where(hit, F32_LOWEST, gate)
      pen_ref[hh] = jnp.where(past, jnp.where(taken > 0.0, 0.0, NEG_INF),
                              jnp.where(jidx == qblk, 0.0, NEG_INF))

  qt = MOBA_Q_BLOCKS * blk
  q = q_ref[0].astype(F32) * (dh ** -0.5)
  qz = [jnp.where(head_lanes(hh, q.shape), q, 0.0).astype(BF16) for hh in range(heads)]
  qcols = pl.ds(pl.multiple_of(i * qt, qt), qt)
  first_qblk = i * MOBA_Q_BLOCKS

  n_blocks = first_qblk + MOBA_Q_BLOCKS
  main_trips = n_blocks // MOBA_STEP_BLOCKS
  tail_start = main_trips * MOBA_STEP_BLOCKS
  tail_trips = (n_blocks - tail_start + MOBA_TAIL_BLOCKS - 1) // MOBA_TAIL_BLOCKS

  def fold(x):
    return x.reshape(blk // sub, sub, qt)

  def score_pass(first_block, blocks):
    def body(t, run_max):
      j0 = first_block + t * blocks
      start = pl.multiple_of(j0 * blk, blocks * blk)
      k = k_ref[0, pl.ds(start, blocks * blk), :]
      out = []
      for hh in range(heads):
        s = _dot_nt(k, qz[hh])
        mx = run_max[hh]
        for u in range(blocks):
          bias = jnp.concatenate(
              [bias_ref[hh, jnp.maximum(first_qblk + qb - (j0 + u), 0)]
               for qb in range(MOBA_Q_BLOCKS)], axis=1)
          part = s[u * blk:(u + 1) * blk] * LOG2E + bias
          s_ref[hh, pl.ds(start + u * blk, blk), :] = part
          mx = jnp.maximum(mx, jnp.max(fold(part), axis=0) + pen_ref[hh, pl.ds(j0 + u, 1), qcols])
        out.append(mx)
      return tuple(out)
    return body

  run_max = tuple(jnp.full((sub, qt), NEG_INF, F32) for _ in range(heads))
  run_max = lax.fori_loop(0, main_trips, score_pass(0, MOBA_STEP_BLOCKS), run_max)
  run_max = lax.fori_loop(0, tail_trips, score_pass(tail_start, MOBA_TAIL_BLOCKS), run_max)
  row_max = [jnp.max(mx, axis=0, keepdims=True) for mx in run_max]

  def value_pass(first_block, blocks):
    def body(t, carry):
      j0 = first_block + t * blocks
      start = pl.multiple_of(j0 * blk, blocks * blk)
      v = v_ref[0, pl.ds(start, blocks * blk), :]
      out = []
      for hh in range(heads):
        lsum, acc = carry[hh]
        probs = []
        for u in range(blocks):
          shift = row_max[hh] - pen_ref[hh, pl.ds(j0 + u, 1), qcols]
          p = jnp.exp2(s_ref[hh, pl.ds(start + u * blk, blk), :] - shift)
          probs.append(p.astype(BF16))
        v_ones = jnp.where(head_lanes(hh, v.shape), v, jnp.ones_like(v))
        pv = _dot_tn(v_ones, jnp.concatenate(probs, axis=0))
        other = (heads - 1 - hh) * dh
        out.append((lsum + pv[other:other + sub], acc + pv[hh * dh:(hh + 1) * dh]))
      return tuple(out)
    return body

  sums = tuple((jnp.zeros((sub, qt), F32), jnp.zeros((dh, qt), F32)) for _ in range(heads))
  sums = lax.fori_loop(0, main_trips, value_pass(0, MOBA_STEP_BLOCKS), sums)
  sums = lax.fori_loop(0, tail_trips, value_pass(tail_start, MOBA_TAIL_BLOCKS), sums)
  for hh, (lsum, acc) in enumerate(sums):
    o_ref[0, hh * dh:(hh + 1) * dh, :] = (acc / lsum[0:1]).astype(o_ref.dtype)


def _moba(proj3, rel_bias):
  b, s, _ = proj3.shape
  nb = s // MOBA_BLOCK
  assert nb % MOBA_Q_BLOCKS == 0
  assert nb % MOBA_TAIL_BLOCKS == 0, "the tail step may touch blocks past the tile's last one"
  pairs = MOBA_W // V7X_LANES
  per_pair = V7X_LANES // MOBA_HEAD_DIM
  qt = MOBA_Q_BLOCKS * MOBA_BLOCK
  blocks = (2 * _nbytes((qt, V7X_LANES), BF16) + 3 * _nbytes((s, V7X_LANES), BF16)
            + _nbytes((nb, 2 * MOBA_BLOCK), jnp.int32))
  scratch = (_nbytes((per_pair, nb, MOBA_BLOCK, MOBA_BLOCK), F32) + _nbytes((per_pair, nb, s), F32)
             + _nbytes((per_pair, s, qt), F32))
  seq_blk = lambda off: pl.BlockSpec((1, s, V7X_LANES), lambda hp, bi, i: (bi, 0, off + hp))
  return pl.pallas_call(
      functools.partial(_moba_kernel, nb=nb),
      out_shape=jax.ShapeDtypeStruct((b, MOBA_W, s), BF16),
      grid=(pairs, b, nb // MOBA_Q_BLOCKS),
      in_specs=[
          pl.BlockSpec(memory_space=pltpu.SMEM),
          pl.BlockSpec((nb, 2 * MOBA_BLOCK), lambda hp, bi, i: (0, 0)),
          pl.BlockSpec((1, qt, V7X_LANES), lambda hp, bi, i: (bi, i, _QA_BLK + hp)),
          seq_blk(_QA_BLK), seq_blk(_KA_BLK), seq_blk(_VA_BLK),
      ],
      out_specs=pl.BlockSpec((1, V7X_LANES, qt), lambda hp, bi, i: (bi, hp, i)),
      scratch_shapes=[
          pltpu.VMEM((per_pair, nb, MOBA_BLOCK, MOBA_BLOCK), F32),
          pltpu.VMEM((per_pair, nb, s), F32),
          pltpu.VMEM((per_pair, s, qt), F32),
      ],
      compiler_params=_compiler_params(("arbitrary", "arbitrary", "arbitrary"), blocks, scratch),
      name="moba",
  )(rel_bias, _bias_buckets(nb), proj3, proj3, proj3, proj3)


def _retention_kernel(q_ref, k_ref, v_ref, g_ref, cos_ref, sin_ref, decay_ref, qin_ref, kout_ref,
                      cdec_ref, gain_ref, o_ref, state_ref):
  c_len = RET_CHUNK
  half = RET_DIM // 2

  @pl.when(pl.program_id(1) == 0)
  def _():
    state_ref[...] = jnp.zeros(state_ref.shape, F32)

  for c in range(RET_ROWS // c_len):
    rows = slice(c * c_len, (c + 1) * c_len)
    cos = cos_ref[rows, :]
    sin = sin_ref[rows, :]
    for h in range(RET_HEADS):
      lanes = slice(h * RET_DIM, (h + 1) * RET_DIM)
      q = q_ref[0, rows, lanes].astype(F32)
      k = k_ref[0, rows, lanes].astype(F32)
      v = v_ref[0, rows, lanes]
      qr = q * cos + pltpu.roll(q, half, 1) * sin
      kr = (k * cos + pltpu.roll(k, half, 1) * sin) * (RET_DIM ** -0.5)
      state = state_ref[h]
      sc = _dot_nt(qr.astype(BF16), kr.astype(BF16)) * decay_ref[h]
      inner = _dot(sc.astype(BF16), v)
      cross = _dot((qr * qin_ref[h]).astype(BF16), state.astype(BF16))
      kv = _dot_tn((kr * kout_ref[h]).astype(BF16), v)
      state_ref[h] = cdec_ref[h] * state + kv
      y = inner + cross
      mu = jnp.mean(y, axis=-1, keepdims=True)
      yc = y - mu
      var = jnp.mean(yc * yc, axis=-1, keepdims=True)
      yn = yc * lax.rsqrt(var + EPS) * gain_ref[:, lanes]
      g = g_ref[0, rows, lanes].astype(F32)
      o_ref[0, rows, lanes] = (yn * (g * jax.nn.sigmoid(g))).astype(o_ref.dtype)


def _retention(proj3, gn_gain):
  b, s, _ = proj3.shape
  c_len = RET_CHUNK
  half = RET_DIM // 2
  pos = jnp.arange(s, dtype=F32)
  inv = ROPE_BASE ** (-jnp.arange(half, dtype=F32) / half)
  ang = pos[:, None] * inv
  cos = jnp.concatenate([jnp.cos(ang), jnp.cos(ang)], axis=-1)
  sin = jnp.concatenate([-jnp.sin(ang), jnp.sin(ang)], axis=-1)
  log_gamma = jnp.log1p(-jnp.power(2.0, -5.0 - jnp.arange(RET_HEADS, dtype=F32)))
  idx = jnp.arange(c_len, dtype=F32)
  diff = idx[:, None] - idx[None, :]
  decay = jnp.where(diff >= 0, jnp.exp(log_gamma[:, None, None] * jnp.maximum(diff, 0.0)), 0.0)
  lanes = (RET_HEADS, c_len, RET_DIM)
  q_in = jnp.broadcast_to(jnp.exp(log_gamma[:, None] * (idx + 1.0))[:, :, None], lanes)
  k_out = jnp.broadcast_to(jnp.exp(log_gamma[:, None] * (c_len - 1.0 - idx))[:, :, None], lanes)
  cdec = jnp.broadcast_to(jnp.exp(log_gamma * c_len)[:, None, None], (RET_HEADS, 1, RET_DIM))

  first = _QR_BLK * V7X_LANES // RET_W
  row_blk = lambda off: pl.BlockSpec((1, RET_ROWS, RET_W), lambda bi, t: (bi, t, first + off))
  head_blk = lambda shape: pl.BlockSpec((RET_HEADS,) + shape, lambda bi, t: (0, 0, 0))
  blocks = (5 * _nbytes((RET_ROWS, RET_W), BF16) + 2 * _nbytes((RET_ROWS, RET_DIM), F32)
            + 3 * _nbytes((RET_HEADS, c_len, RET_DIM), F32))
  return pl.pallas_call(
      _retention_kernel,
      out_shape=jax.ShapeDtypeStruct((b, s, RET_W), BF16),
      grid=(b, s // RET_ROWS),
      in_specs=[
          row_blk(0), row_blk(1), row_blk(2), row_blk(3),
          pl.BlockSpec((RET_ROWS, RET_DIM), lambda bi, t: (t, 0)),
          pl.BlockSpec((RET_ROWS, RET_DIM), lambda bi, t: (t, 0)),
          head_blk((c_len, c_len)), head_blk((c_len, RET_DIM)), head_blk((c_len, RET_DIM)),
          head_blk((1, RET_DIM)),
          pl.BlockSpec((1, RET_W), lambda bi, t: (0, 0)),
      ],
      out_specs=pl.BlockSpec((1, RET_ROWS, RET_W), lambda bi, t: (bi, t, 0)),
      scratch_shapes=[pltpu.VMEM((RET_HEADS, RET_DIM, RET_DIM), F32)],
      compiler_params=_compiler_params(("parallel", "arbitrary"), blocks,
                                       _nbytes((RET_HEADS, RET_DIM, RET_DIM), F32)),
      name="retention",
  )(proj3, proj3, proj3, proj3, cos, sin, decay, q_in, k_out, cdec, gn_gain)


def _mem_kv_kernel(m_ref, g_ref, w_ref, o_ref):
  n = _rmsnorm(m_ref[0], g_ref[...]).astype(BF16)
  o_ref[0] = _dot(n, w_ref[...]).astype(BF16)


def _mem_kv(mem, g, w):
  b = mem.shape[0]
  blocks = (_nbytes((MEM_LEN, D_MODEL), F32) + _nbytes((D_MODEL, 2 * MEM_W), BF16)
            + _nbytes((MEM_LEN, 2 * MEM_W), BF16))
  return pl.pallas_call(
      _mem_kv_kernel,
      out_shape=jax.ShapeDtypeStruct((b, MEM_LEN, 2 * MEM_W), BF16),
      grid=(b,),
      in_specs=[
          pl.BlockSpec((1, MEM_LEN, D_MODEL), lambda i: (i, 0, 0)),
          pl.BlockSpec((1, D_MODEL), lambda i: (0, 0)),
          pl.BlockSpec((D_MODEL, 2 * MEM_W), lambda i: (0, 0)),
      ],
      out_specs=pl.BlockSpec((1, MEM_LEN, 2 * MEM_W), lambda i: (i, 0, 0)),
      compiler_params=_compiler_params(("parallel",), blocks),
      name="mem_kv",
  )(mem, g, w)


def _mem_attention(q_ref, kv_ref):
  dh = MEM_HEAD_DIM
  outs = []
  for h in range(MEM_HEADS):
    q = q_ref[0, :, h * dh:(h + 1) * dh]
    k = kv_ref[0, :, h * dh:(h + 1) * dh]
    v = kv_ref[0, :, MEM_W + h * dh:MEM_W + (h + 1) * dh]
    s = _dot_nt(q, k) * (dh ** -0.5)
    p = jnp.exp(s - jnp.max(s, axis=-1, keepdims=True))
    l = jnp.sum(p, axis=-1, keepdims=True)
    outs.append(_dot(p.astype(BF16), v) / l)
  return jnp.concatenate(outs, axis=1).astype(BF16)


def _merge_kernel(x_ref, ya_ref, yr_ref, qm_ref, kv_ref, za_ref, zr_ref, zm_ref, wa_ref, wr_ref,
                  wm_ref, wo_ref, o_ref):
  def gate(z_ref, proj):
    return jax.nn.sigmoid(z_ref[0].astype(F32)) * proj

  merged = (gate(za_ref, _dot_tn(ya_ref[0], wa_ref[...]))
            + gate(zr_ref, _dot(yr_ref[0], wr_ref[...]))
            + gate(zm_ref, _dot(_mem_attention(qm_ref, kv_ref), wm_ref[...])))
  o_ref[0] = x_ref[0] + _dot(merged.astype(BF16), wo_ref[...])


def _merge(x, ya_t, yr, kv, proj3, wa, wr, wm, wo):
  b, s, _ = x.shape
  zblk = _Z_COL // D_MODEL
  row = lambda w, col=0: pl.BlockSpec((1, ROW_TILE, w), lambda bi, t: (bi, t, col))
  ya_spec = pl.BlockSpec((1, MOBA_W, ROW_TILE), lambda bi, t: (bi, 0, t))
  full = lambda shape: pl.BlockSpec(shape, lambda bi, t: (0, 0))
  blocks = (2 * _nbytes((ROW_TILE, D_MODEL), F32) + 3 * _nbytes((ROW_TILE, MOBA_W), BF16)
            + 3 * _nbytes((ROW_TILE, D_MODEL), BF16) + 3 * _nbytes((MOBA_W, D_MODEL), BF16)
            + _nbytes((D_MODEL, D_MODEL), BF16) + _nbytes((MEM_LEN, 2 * MEM_W), BF16))
  return pl.pallas_call(
      _merge_kernel,
      out_shape=jax.ShapeDtypeStruct((b, s, D_MODEL), F32),
      grid=(b, s // ROW_TILE),
      in_specs=[
          row(D_MODEL), ya_spec, row(RET_W), row(MEM_W, _QM_COL // MEM_W),
          pl.BlockSpec((1, MEM_LEN, 2 * MEM_W), lambda bi, t: (bi, 0, 0)),
          row(D_MODEL, zblk), row(D_MODEL, zblk + 1), row(D_MODEL, zblk + 2),
          full((MOBA_W, D_MODEL)), full((RET_W, D_MODEL)), full((MEM_W, D_MODEL)),
          full((D_MODEL, D_MODEL)),
      ],
      out_specs=row(D_MODEL),
      compiler_params=_compiler_params(("parallel", "parallel"), blocks),
      name="merge",
  )(x, ya_t, yr, proj3, kv, proj3, proj3, proj3, wa, wr, wm, wo)


def _ffn_kernel(h_ref, gf_ref, wup_ref, cw_ref, cb_ref, wdn_ref, gfin_ref, o_ref, hid_ref,
                act_ref):
  tm = ROW_TILE
  halo = CONV_HALO
  cw = FFN_COL_CHUNK
  n_chunks = D_FF // cw

  @pl.when(pl.program_id(1) == 0)
  def _():
    hid_ref[:, 0:halo, :] = jnp.zeros((2 * n_chunks, halo, cw), F32)

  h1 = h_ref[0]
  n = _rmsnorm(h1, gf_ref[...]).astype(BF16)

  for c in range(n_chunks):
    for part in range(2):
      col = part * D_FF + c * cw
      hid_ref[2 * c + part, halo:halo + tm, :] = _dot(n, wup_ref[:, col:col + cw])

  top_row = lax.broadcasted_iota(jnp.int32, (halo, cw), 0) == 0

  def shift_down(v, top):
    rolled = pltpu.roll(v, 1, 0)
    return jnp.concatenate([jnp.where(top_row, top, rolled[0:halo]), rolled[halo:]], axis=0)

  assert CONV_WIDTH == 3
  acc = None
  done = 0
  for c in range(n_chunks):
    halves = []
    for part in range(2):
      col = part * D_FF + c * cw
      cols = slice(col, col + cw)
      buf = hid_ref.at[2 * c + part]
      x = buf[halo:halo + tm, :]
      last = buf[halo - 1:halo, :]
      before_last = buf[halo - 2:halo - 1, :]
      w0, w1, w2 = (cw_ref[tap:tap + 1, cols] for tap in range(CONV_WIDTH))
      a = shift_down(w0 * x, w0 * last)
      b = shift_down(w1 * x + a, w1 * last + w0 * before_last)
      halves.append((cb_ref[:, cols] + w2 * x) + b)
      buf[0:halo, :] = buf[tm:tm + halo, :]
    gate, up = halves
    act = gate * (1.0 + lax.erf(gate * math.sqrt(0.5))) * up
    act_ref[:, c * cw:(c + 1) * cw] = act.astype(BF16)
    if c + 1 in (FFN_DOWN_SPLIT, n_chunks):
      ks = slice(done * cw, (c + 1) * cw)
      part_sum = _dot(act_ref[:, ks], wdn_ref[ks, :])
      acc = part_sum if acc is None else acc + part_sum
      done = c + 1

  o_ref[0] = _rmsnorm(h1 + acc, gfin_ref[...])


def _ffn(h3, g_ffn, w_up, conv_w, conv_b, w_down, g_final):
  b, s, _ = h3.shape
  full = lambda shape: pl.BlockSpec(shape, lambda bi, t: (0, 0))
  once = lambda shape: pl.BlockSpec(shape, lambda bi, t: (0, 0), pipeline_mode=pl.Buffered(1))
  tiles = 2 * _nbytes((ROW_TILE, D_MODEL), F32) + _nbytes((CONV_WIDTH + 1, 2 * D_FF), F32)
  weights = _nbytes((D_MODEL, 2 * D_FF), BF16) + _nbytes((D_FF, D_MODEL), BF16)
  slots = 2 * D_FF // FFN_COL_CHUNK
  scratch = (_nbytes((slots, ROW_TILE + CONV_HALO, FFN_COL_CHUNK), F32)
             + _nbytes((ROW_TILE, D_FF), BF16))
  return pl.pallas_call(
      _ffn_kernel,
      out_shape=jax.ShapeDtypeStruct((b, s, D_MODEL), F32),
      grid=(b, s // ROW_TILE),
      in_specs=[
          pl.BlockSpec((1, ROW_TILE, D_MODEL), lambda bi, t: (bi, t, 0)),
          full((1, D_MODEL)),
          once((D_MODEL, 2 * D_FF)),
          full((CONV_WIDTH, 2 * D_FF)),
          full((1, 2 * D_FF)),
          once((D_FF, D_MODEL)),
          full((1, D_MODEL)),
      ],
      out_specs=pl.BlockSpec((1, ROW_TILE, D_MODEL), lambda bi, t: (bi, t, 0)),
      scratch_shapes=[
          pltpu.VMEM((slots, ROW_TILE + CONV_HALO, FFN_COL_CHUNK), F32),
          pltpu.VMEM((ROW_TILE, D_FF), BF16),
      ],
      compiler_params=_compiler_params(("parallel", "arbitrary"), tiles, weights + scratch),
      name="ffn",
  )(h3, g_ffn, w_up, conv_w, conv_b, w_down, g_final)


def kernel(x, mem, g_mix, w_in, rel_bias, ret_gn_gain, g_mem, w_mem_kv, w_br_attn, w_br_ret,
           w_br_mem, w_out, g_ffn, w_up, conv_w, conv_b, w_down, g_final):
  b, s, d = x.shape
  assert d == D_MODEL and mem.shape == (b, MEM_LEN, D_MODEL)
  assert s % MOBA_BLOCK == 0 and s % ROW_TILE == 0 and s % RET_ROWS == 0
  assert g_mix.shape[0] == 1, "single-layer block"
  h = x.reshape(b * s, d)
  proj = _in_proj(h, g_mix, w_in[0].astype(BF16))
  proj3 = proj.reshape(b, s, D_IN)
  ya_t = _moba(proj3, rel_bias)
  y_r = _retention(proj3, ret_gn_gain)
  kv = _mem_kv(mem, g_mem, w_mem_kv[0].astype(BF16))
  h1 = _merge(x, ya_t, y_r, kv, proj3, w_br_attn[0].astype(BF16), w_br_ret[0].astype(BF16),
              w_br_mem[0].astype(BF16), w_out[0].astype(BF16))
  return _ffn(h1, g_ffn, w_up[0].astype(BF16), conv_w[0], conv_b,
              (0.5 * w_down[0]).astype(BF16), g_final.reshape(1, d))
```
